```python
import math
import jax, jax.numpy as jnp
from jax import lax
import numpy as np

D_MODEL = 2048
BATCH = 4
SEQ = 2048
DEPTH = 4
DEC_BATCH = 8
DEC_SEQ = 8
PAST_LEN = 16384
PAGE_SIZE = 128

N_MIXERS = 2
N_SWA_LAYERS = (DEPTH + 1) // 2
N_SB_LAYERS = DEPTH // 2
HEAD_DIM = 128
SWA_HEADS = D_MODEL // HEAD_DIM
SWA_WINDOWS = (128, 512, 2048)
SWA_DILATIONS = (1, 4, 16)
N_SWA_GROUPS = 3
SWA_SPAN = 128
Q_BLOCK = 128
SB_HEADS = D_MODEL // HEAD_DIM
SB_BIAS_HI = -5.0
SB_BIAS_LO = -9.0
D_FF = -(-(8 * D_MODEL) // (3 * 256)) * 256
EPS = 1e-6
MASK_VALUE = -1e30

kernel_name = "hybrid_dilated_stickbreak_decoder_step"


def rmsnorm(x, g):
    xf = x.astype(jnp.float32)
    y = xf * lax.rsqrt(jnp.mean(xf * xf, axis=-1, keepdims=True) + EPS)
    return (y * g.astype(jnp.float32)).astype(x.dtype)


def ada_mod(c, w, b):
    mod = jax.nn.silu(c) @ w + b
    return [m[:, None, :] for m in jnp.split(mod, 6, axis=-1)]


def modulate(h, shift, scale):
    return h * (1 + scale) + shift


def swiglu(h, wg, wu, wd):
    return (jax.nn.silu(h @ wg) * (h @ wu)) @ wd


def alibi_slopes():
    n = N_SWA_GROUPS * SWA_HEADS
    s = 2.0 ** (-8.0 * np.arange(1, n + 1) / n)
    return jnp.asarray(s.reshape(N_SWA_GROUPS, SWA_HEADS), dtype=jnp.float32)


def attend(s, v, eq):
    m = jnp.max(s, axis=-1, keepdims=True)
    p = jnp.exp(s - m)
    den = jnp.sum(p, axis=-1, keepdims=True)
    o = jnp.einsum(eq, p / den, v)
    return o, (m + jnp.log(den))[..., 0]


def dilated_window_prompt(q, k, v, dil, slopes):
    B, S, H, E = q.shape
    L = S // dil
    nb = -(-L // Q_BLOCK)
    Lp = nb * Q_BLOCK

    def strided(t):
        t = t.reshape(B, L, dil, H, E).transpose(0, 2, 1, 3, 4)
        return jnp.pad(t, ((0, 0), (0, 0), (0, Lp - L), (0, 0), (0, 0)))

    def key_blocks(t):
        t = jnp.pad(strided(t), ((0, 0), (0, 0), (Q_BLOCK, 0), (0, 0), (0, 0)))
        t = t.reshape(B, dil, nb + 1, Q_BLOCK, H, E)
        return jnp.concatenate([t[:, :, :-1], t[:, :, 1:]], axis=3)

    qb = strided(q).reshape(B, dil, nb, Q_BLOCK, H, E)
    kb, vb = key_blocks(k), key_blocks(v)
    qi = jnp.arange(Q_BLOCK)[:, None]
    kj = jnp.arange(2 * Q_BLOCK)[None, :]
    delta = Q_BLOCK + qi - kj
    key_idx = jnp.arange(nb)[:, None, None] * Q_BLOCK - Q_BLOCK + kj
    valid = (delta >= 0) & (delta <= SWA_SPAN) & (key_idx >= 0)
    s = jnp.einsum('brnqhe,brnkhe->brnhqk', qb, kb) * (E ** -0.5)
    s = s - slopes[:, None, None] * (delta * dil).astype(jnp.float32)
    s = jnp.where(valid[:, None], s, MASK_VALUE)
    o, lse = attend(s, vb, 'brnhqk,brnkhe->brnqhe')
    o = o.reshape(B, dil, Lp, H, E)[:, :, :L].transpose(0, 2, 1, 3, 4).reshape(B, S, H, E)
    lse = lse.transpose(0, 1, 2, 4, 3).reshape(B, dil, Lp, H)[:, :, :L]
    lse = lse.transpose(0, 2, 1, 3).reshape(B, S, H)
    return o, lse


def dilated_window_sample(q, k_all, v_all, pos_all, dil, slopes):
    T, E = q.shape[1], q.shape[3]
    W = k_all.shape[1] - T
    jj = jnp.arange(SWA_SPAN + 1)
    idx = W + jnp.arange(T)[:, None] - jj[None, :] * dil
    kg, vg = k_all[:, idx], v_all[:, idx]
    s = jnp.einsum('bqhe,bqjhe->bhqj', q, kg) * (E ** -0.5)
    s = s - slopes[:, None, None] * (jj * dil).astype(jnp.float32)
    s = jnp.where(pos_all[idx] >= 0, s, MASK_VALUE)
    o, lse = attend(s, vg, 'bhqj,bqjhe->bqhe')
    return o, lse.transpose(0, 2, 1)


def merge_groups(outs, lses, w_o, dtype):
    wts = jax.nn.softmax(jnp.stack(lses), axis=0)
    o = jnp.einsum('gbsh,gbshe->bshe', wts, jnp.stack(outs))
    B, S = o.shape[:2]
    return o.reshape(B, S, -1).astype(dtype) @ w_o


def swa_project(h, w_qkv):
    B, S, _ = h.shape
    return (h @ w_qkv).astype(jnp.float32).reshape(B, S, N_SWA_GROUPS, 3, SWA_HEADS, HEAD_DIM)


def last_rows(t, n):
    return jnp.pad(t, ((0, 0), (n, 0), (0, 0), (0, 0)))[:, -n:]


def swa_prompt(h, w_qkv, w_o, slopes):
    qkv = swa_project(h, w_qkv)
    outs, lses, bufs = [], [], []
    for g in range(N_SWA_GROUPS):
        q, k, v = qkv[:, :, g, 0], qkv[:, :, g, 1], qkv[:, :, g, 2]
        o, lse = dilated_window_prompt(q, k, v, SWA_DILATIONS[g], slopes[g])
        outs.append(o)
        lses.append(lse)
        W = SWA_WINDOWS[g]
        bufs.append((last_rows(k, W).astype(h.dtype), last_rows(v, W).astype(h.dtype)))
    return merge_groups(outs, lses, w_o, h.dtype), bufs


def swa_sample(h, past, past_len, w_qkv, w_o, slopes):
    T = h.shape[1]
    qkv = swa_project(h, w_qkv)
    outs, lses, bufs = [], [], []
    for g in range(N_SWA_GROUPS):
        q, k, v = qkv[:, :, g, 0], qkv[:, :, g, 1], qkv[:, :, g, 2]
        W = SWA_WINDOWS[g]
        k_all = jnp.concatenate([past[g][0].astype(jnp.float32), k], axis=1)
        v_all = jnp.concatenate([past[g][1].astype(jnp.float32), v], axis=1)
        pos_all = past_len - W + jnp.arange(W + T)
        o, lse = dilated_window_sample(q, k_all, v_all, pos_all, SWA_DILATIONS[g], slopes[g])
        outs.append(o)
        lses.append(lse)
        bufs.append((k_all[:, -W:].astype(h.dtype), v_all[:, -W:].astype(h.dtype)))
    return merge_groups(outs, lses, w_o, h.dtype), bufs


def stick_breaking_weights(z, mask, carry):
    log_keep = jnp.where(mask, jax.nn.log_sigmoid(-z), 0.0)
    newer = lax.cumsum(log_keep, axis=z.ndim - 1, reverse=True) - log_keep + carry[..., None]
    a = jnp.where(mask, jnp.exp(jax.nn.log_sigmoid(z) + newer), 0.0)
    return a, carry + jnp.sum(log_keep, axis=-1)


def sb_project(h, w_qkv):
    B, S, _ = h.shape
    qkv = (h @ w_qkv).astype(jnp.float32).reshape(B, S, 3, SB_HEADS, HEAD_DIM)
    return qkv[:, :, 0] * (HEAD_DIM ** -0.5), qkv[:, :, 1], qkv[:, :, 2]


def sb_logits(q, k, bias):
    return jnp.einsum('bqhe,bkhe->bhqk', q, k) + bias.astype(jnp.float32)[None, :, None, None]


def stick_breaking_prompt(q, k, v, bias):
    B, S, H, E = q.shape
    nb = S // Q_BLOCK
    qb = q.reshape(B, nb, Q_BLOCK, H, E).transpose(1, 0, 2, 3, 4)
    kpos = jnp.arange(S)
    qpos = kpos.reshape(nb, Q_BLOCK)

    def block(args):
        qi, pi = args
        z = sb_logits(qi, k, bias)
        mask = kpos[None, :] < pi[:, None]
        a, _ = stick_breaking_weights(z, mask, jnp.zeros(z.shape[:-1], jnp.float32))
        return jnp.einsum('bhqk,bkhe->bqhe', a, v)

    o = lax.map(block, (qb, qpos))
    return o.transpose(1, 0, 2, 3, 4).reshape(B, S, H, E)


def stick_breaking_sample(q, k_new, v_new, cache_k, cache_v, layer, page_table, bias):
    DB, T, H, E = q.shape
    t = jnp.arange(T)
    z = sb_logits(q, k_new, bias)
    a, carry = stick_breaking_weights(z, t[None, :] < t[:, None], jnp.zeros((DB, H, T), jnp.float32))
    o = jnp.einsum('bhqk,bkhe->bqhe', a, v_new)
    page_mask = jnp.ones((T, PAGE_SIZE), dtype=bool)

    def step(state, pages):
        o_acc, c = state
        kp = cache_k[layer, pages].astype(jnp.float32)
        vp = cache_v[layer, pages].astype(jnp.float32)
        zp = sb_logits(q, kp, bias)
        ap, c = stick_breaking_weights(zp, page_mask, c)
        return (o_acc + jnp.einsum('bhqk,bkhe->bqhe', ap, vp), c), None

    (o, _), _ = lax.scan(step, (o, carry), page_table.T[::-1])
    return o


def sb_prompt_mixer(h, w_qkv, w_o, bias):
    B, S, _ = h.shape
    q, k, v = sb_project(h, w_qkv)
    o = stick_breaking_prompt(q, k, v, bias)
    return o.reshape(B, S, -1).astype(h.dtype) @ w_o, (k.astype(h.dtype), v.astype(h.dtype))


def sb_sample_mixer(h, cache_k, cache_v, layer, page_table, w_qkv, w_o, bias):
    DB, T, _ = h.shape
    q, k, v = sb_project(h, w_qkv)
    o = stick_breaking_sample(q, k, v, cache_k, cache_v, layer, page_table, bias)
    return o.reshape(DB, T, -1).astype(h.dtype) @ w_o, (k.astype(h.dtype), v.astype(h.dtype))


def setup_inputs(seed: int = 0) -> dict:
    key = jax.random.key(seed)
    ks = jax.random.split(key, 32)
    f32 = jnp.float32
    n_pages = PAST_LEN // PAGE_SIZE
    n_used = DEC_BATCH * n_pages
    n_pool = n_used + n_used // 4
    hd = (SWA_HEADS, HEAD_DIM)

    def nrm(k, shape, scale=1.0):
        return jax.random.normal(k, shape, f32) * scale

    page_table = jax.random.permutation(ks[8], n_pool)[:n_used].reshape(DEC_BATCH, n_pages).astype(jnp.int32)
    sb_bias = jnp.linspace(SB_BIAS_HI, SB_BIAS_LO, SB_HEADS, dtype=f32)[None, :] + nrm(ks[25], (N_SB_LAYERS, SB_HEADS), 0.1)
    D = D_MODEL
    return {
        'x_prompt': nrm(ks[0], (BATCH, SEQ, D)),
        'x_sample': nrm(ks[1], (DEC_BATCH, DEC_SEQ, D)),
        'cache_swa_k0': nrm(ks[2], (N_SWA_LAYERS, DEC_BATCH, SWA_WINDOWS[0]) + hd),
        'cache_swa_v0': nrm(ks[3], (N_SWA_LAYERS, DEC_BATCH, SWA_WINDOWS[0]) + hd),
        'cache_swa_k1': nrm(ks[4], (N_SWA_LAYERS, DEC_BATCH, SWA_WINDOWS[1]) + hd),
        'cache_swa_v1': nrm(ks[5], (N_SWA_LAYERS, DEC_BATCH, SWA_WINDOWS[1]) + hd),
        'cache_swa_k2': nrm(ks[6], (N_SWA_LAYERS, DEC_BATCH, SWA_WINDOWS[2]) + hd),
        'cache_swa_v2': nrm(ks[7], (N_SWA_LAYERS, DEC_BATCH, SWA_WINDOWS[2]) + hd),
        'cache_sb_k': nrm(ks[9], (N_SB_LAYERS, n_pool, PAGE_SIZE, SB_HEADS, HEAD_DIM)),
        'cache_sb_v': nrm(ks[10], (N_SB_LAYERS, n_pool, PAGE_SIZE, SB_HEADS, HEAD_DIM)),
        'page_table': page_table,
        'c_prompt': nrm(ks[11], (BATCH, D)),
        'c_sample': nrm(ks[12], (DEC_BATCH, D)),
        'w_ada': nrm(ks[13], (DEPTH, D, 6 * D), D ** -0.5),
        'b_ada': nrm(ks[14], (DEPTH, 6 * D), 0.02),
        'norm_mix_g': 1.0 + nrm(ks[15], (DEPTH, D), 0.02),
        'norm_ffn_g': 1.0 + nrm(ks[16], (DEPTH, D), 0.02),
        'w_qkv_swa': nrm(ks[17], (N_SWA_LAYERS, D, N_SWA_GROUPS * 3 * SWA_HEADS * HEAD_DIM), D ** -0.5),
        'w_o_swa': nrm(ks[18], (N_SWA_LAYERS, SWA_HEADS * HEAD_DIM, D), (SWA_HEADS * HEAD_DIM) ** -0.5),
        'w_qkv_sb': nrm(ks[19], (N_SB_LAYERS, D, 3 * SB_HEADS * HEAD_DIM), D ** -0.5),
        'w_o_sb': nrm(ks[20], (N_SB_LAYERS, SB_HEADS * HEAD_DIM, D), (SB_HEADS * HEAD_DIM) ** -0.5),
        'sb_bias': sb_bias,
        'w_gate': nrm(ks[21], (DEPTH, D, D_FF), D ** -0.5),
        'w_up': nrm(ks[22], (DEPTH, D, D_FF), D ** -0.5),
        'w_down': nrm(ks[23], (DEPTH, D_FF, D), D_FF ** -0.5),
        'final_norm_g': 1.0 + nrm(ks[24], (D,), 0.02),
    }


def reference(x_prompt, x_sample, cache_swa_k0, cache_swa_v0, cache_swa_k1, cache_swa_v1,
              cache_swa_k2, cache_swa_v2, cache_sb_k, cache_sb_v, page_table, c_prompt, c_sample,
              w_ada, b_ada, norm_mix_g, norm_ffn_g, w_qkv_swa, w_o_swa, w_qkv_sb, w_o_sb,
              sb_bias, w_gate, w_up, w_down, final_norm_g):
    past_len = page_table.shape[1] * cache_sb_k.shape[2]
    slopes = alibi_slopes()
    swa_caches = ((cache_swa_k0, cache_swa_v0), (cache_swa_k1, cache_swa_v1), (cache_swa_k2, cache_swa_v2))
    swa_kp = [[] for _ in range(N_SWA_GROUPS)]
    swa_vp = [[] for _ in range(N_SWA_GROUPS)]
    swa_ks = [[] for _ in range(N_SWA_GROUPS)]
    swa_vs = [[] for _ in range(N_SWA_GROUPS)]
    sb_kp, sb_vp, sb_ks, sb_vs = [], [], [], []
    xp, xs = x_prompt, x_sample
    for i in range(DEPTH):
        mp = ada_mod(c_prompt, w_ada[i], b_ada[i])
        ms = ada_mod(c_sample, w_ada[i], b_ada[i])
        hp = modulate(rmsnorm(xp, norm_mix_g[i]), mp[0], mp[1])
        hs = modulate(rmsnorm(xs, norm_mix_g[i]), ms[0], ms[1])
        if i % N_MIXERS == 0:
            a = i // N_MIXERS
            op, bp = swa_prompt(hp, w_qkv_swa[a], w_o_swa[a], slopes)
            past = [(swa_caches[g][0][a], swa_caches[g][1][a]) for g in range(N_SWA_GROUPS)]
            os_, bs = swa_sample(hs, past, past_len, w_qkv_swa[a], w_o_swa[a], slopes)
            for g in range(N_SWA_GROUPS):
                swa_kp[g].append(bp[g][0])
                swa_vp[g].append(bp[g][1])
                swa_ks[g].append(bs[g][0])
                swa_vs[g].append(bs[g][1])
        else:
            b = i // N_MIXERS
            op, (kp, vp) = sb_prompt_mixer(hp, w_qkv_sb[b], w_o_sb[b], sb_bias[b])
            os_, (ks_, vs_) = sb_sample_mixer(hs, cache_sb_k, cache_sb_v, b, page_table,
                                              w_qkv_sb[b], w_o_sb[b], sb_bias[b])
            sb_kp.append(kp)
            sb_vp.append(vp)
            sb_ks.append(ks_)
            sb_vs.append(vs_)
        xp = xp + mp[2] * op
        xs = xs + ms[2] * os_
        hp = modulate(rmsnorm(xp, norm_ffn_g[i]), mp[3], mp[4])
        hs = modulate(rmsnorm(xs, norm_ffn_g[i]), ms[3], ms[4])
        xp = xp + mp[5] * swiglu(hp, w_gate[i], w_up[i], w_down[i])
        xs = xs + ms[5] * swiglu(hs, w_gate[i], w_up[i], w_down[i])
    y_prompt = rmsnorm(xp, final_norm_g)
    y_sample = rmsnorm(xs, final_norm_g)
    swa_k0_p, swa_k1_p, swa_k2_p = [jnp.stack(t) for t in swa_kp]
    swa_v0_p, swa_v1_p, swa_v2_p = [jnp.stack(t) for t in swa_vp]
    swa_k0_s, swa_k1_s, swa_k2_s = [jnp.stack(t) for t in swa_ks]
    swa_v0_s, swa_v1_s, swa_v2_s = [jnp.stack(t) for t in swa_vs]
    sb_k_p, sb_v_p = jnp.stack(sb_kp), jnp.stack(sb_vp)
    sb_k_s, sb_v_s = jnp.stack(sb_ks), jnp.stack(sb_vs)
    return (y_prompt, y_sample,
            swa_k0_p, swa_v0_p, swa_k1_p, swa_v1_p, swa_k2_p, swa_v2_p, sb_k_p, sb_v_p,
            swa_k0_s, swa_v0_s, swa_k1_s, swa_v1_s, swa_k2_s, swa_v2_s, sb_k_s, sb_v_s)
```

```python
import functools
import math

import numpy as np
import jax
import jax.numpy as jnp
from jax import lax
from jax.experimental import pallas as pl
from jax.experimental.pallas import tpu as pltpu

F32 = jnp.float32
BF16 = jnp.bfloat16

HEAD_DIM = 128
SWA_WINDOWS = (128, 512, 2048)
SWA_DILATIONS = (1, 4, 16)
SWA_SPAN = 128
Q_BLOCK = 128
PAGE_SIZE = 128
EPS = 1e-6
MASK_VALUE = -1e30
LANES = 128
SUBLANES = 8
VMEM_LIMIT_BYTES = 56 * 1024 * 1024

NT_DIMS = (((1,), (1,)), ((), ()))
TN_DIMS = (((0,), (0,)), ((), ()))


def _params(*sem):
    return pltpu.CompilerParams(dimension_semantics=sem, vmem_limit_bytes=VMEM_LIMIT_BYTES)


def _dot(a, b, dims=None):
    if dims is None:
        return jnp.dot(a, b, preferred_element_type=F32)
    return lax.dot_general(a, b, dims, preferred_element_type=F32)


def _softplus(z):
    return jnp.maximum(z, 0.0) + jnp.log1p(jnp.exp(-jnp.abs(z)))


def _split_bf16(x):
    hi = x.astype(BF16)
    lo = (x - hi.astype(F32)).astype(BF16)
    return hi, lo


def _ada_kernel(c_ref, w_ref, b_ref, o_ref):
    c = c_ref[...]
    a = (c / (1.0 + jnp.exp(-c))).astype(BF16)
    o_ref[...] = _dot(a, w_ref[...].astype(BF16)) + b_ref[...]


def ada_all(c_all, w_ada, b_ada, tn=1024):
    depth, d, n = w_ada.shape
    r = c_all.shape[0]
    return pl.pallas_call(
        _ada_kernel,
        grid=(depth, n // tn),
        in_specs=[
            pl.BlockSpec((r, d), lambda l, j: (0, 0)),
            pl.BlockSpec((None, d, tn), lambda l, j: (l, 0, j)),
            pl.BlockSpec((None, 1, tn), lambda l, j: (l, 0, j)),
        ],
        out_specs=pl.BlockSpec((None, r, tn), lambda l, j: (l, 0, j)),
        out_shape=jax.ShapeDtypeStruct((depth, r, n), F32),
        compiler_params=_params("arbitrary", "arbitrary"),
        name="ada_mod",
    )(c_all, w_ada, b_ada.reshape(depth, 1, n))


def _rms(x):
    return x * lax.rsqrt(jnp.mean(x * x, axis=-1, keepdims=True) + EPS)


def _norm_mod_kernel(x_ref, g_ref, sh_ref, sc_ref, o_ref):
    y = _rms(x_ref[...]) * g_ref[...]
    o_ref[...] = (y * (1.0 + sc_ref[...]) + sh_ref[...]).astype(o_ref.dtype)


def _norm_kernel(x_ref, g_ref, o_ref):
    o_ref[...] = (_rms(x_ref[...]) * g_ref[...]).astype(o_ref.dtype)


def _mod_spec(mod, tiles_per_group, width, two_d):
    r = mod.shape[1]
    if two_d:
        return pl.BlockSpec((None, r, width), lambda i, j: (i // tiles_per_group, 0, j))
    return pl.BlockSpec((None, r, width), lambda i: (i // tiles_per_group, 0, 0))


def norm_mod(x, g, shift, scale, tm):
    m, d = x.shape
    tpg = (m // tm) // shift.shape[0]
    return pl.pallas_call(
        _norm_mod_kernel,
        grid=(m // tm,),
        in_specs=[
            pl.BlockSpec((tm, d), lambda i: (i, 0)),
            pl.BlockSpec((1, d), lambda i: (0, 0)),
            _mod_spec(shift, tpg, d, False),
            _mod_spec(scale, tpg, d, False),
        ],
        out_specs=pl.BlockSpec((tm, d), lambda i: (i, 0)),
        out_shape=jax.ShapeDtypeStruct((m, d), BF16),
        compiler_params=_params("arbitrary"),
        name="norm_mod",
    )(x, g.reshape(1, d), shift, scale)


def final_norm(x, g, tm):
    m, d = x.shape
    return pl.pallas_call(
        _norm_kernel,
        grid=(m // tm,),
        in_specs=[pl.BlockSpec((tm, d), lambda i: (i, 0)), pl.BlockSpec((1, d), lambda i: (0, 0))],
        out_specs=pl.BlockSpec((tm, d), lambda i: (i, 0)),
        out_shape=jax.ShapeDtypeStruct((m, d), F32),
        compiler_params=_params("arbitrary"),
        name="final_norm",
    )(x, g.reshape(1, d))


def _proj_kernel(a_ref, w_ref, o_ref, *, scaled_tiles, scale):
    acc = _dot(a_ref[...].astype(BF16), w_ref[...].astype(BF16))
    if scaled_tiles:
        acc = acc * jnp.where(pl.program_id(1) < scaled_tiles, scale, 1.0).astype(F32)
    o_ref[...] = acc


def project_segments(a, w, layer, seg, tm, tn, scaled_seg0=None):
    m, k = a.shape
    n = w.shape[2]
    tps = seg // tn
    kern = functools.partial(
        _proj_kernel,
        scaled_tiles=tps if scaled_seg0 is not None else 0,
        scale=scaled_seg0 if scaled_seg0 is not None else 1.0,
    )
    return pl.pallas_call(
        kern,
        grid=(m // tm, n // tn),
        in_specs=[
            pl.BlockSpec((tm, k), lambda i, j: (i, 0)),
            pl.BlockSpec((None, k, tn), lambda i, j: (layer, 0, j)),
        ],
        out_specs=pl.BlockSpec((None, tm, tn), lambda i, j: (j // tps, i, j % tps)),
        out_shape=jax.ShapeDtypeStruct((n // seg, m, seg), F32),
        compiler_params=_params("arbitrary", "arbitrary"),
        name="project_segments",
    )(a, w)


def _residual_kernel(a_ref, w_ref, x_ref, g_ref, o_ref):
    acc = _dot(a_ref[...].astype(BF16), w_ref[...].astype(BF16))
    o_ref[...] = x_ref[...] + g_ref[...] * acc


def project_residual(a, w, layer, x, gate, tm, tn):
    m, k = a.shape
    n = w.shape[2]
    tpg = (m // tm) // gate.shape[0]
    return pl.pallas_call(
        _residual_kernel,
        grid=(m // tm, n // tn),
        in_specs=[
            pl.BlockSpec((tm, k), lambda i, j: (i, 0)),
            pl.BlockSpec((None, k, tn), lambda i, j: (layer, 0, j)),
            pl.BlockSpec((tm, tn), lambda i, j: (i, j)),
            _mod_spec(gate, tpg, tn, True),
        ],
        out_specs=pl.BlockSpec((tm, tn), lambda i, j: (i, j)),
        out_shape=jax.ShapeDtypeStruct((m, n), F32),
        compiler_params=_params("arbitrary", "arbitrary"),
        name="project_residual",
    )(a, w, x, gate)


def _swiglu_kernel(a_ref, wg_ref, wu_ref, o_ref):
    a = a_ref[...].astype(BF16)
    g = _dot(a, wg_ref[...].astype(BF16))
    u = _dot(a, wu_ref[...].astype(BF16))
    o_ref[...] = ((g / (1.0 + jnp.exp(-g))) * u).astype(o_ref.dtype)


def project_swiglu(a, w_gate, w_up, layer, tm, tn):
    m, k = a.shape
    n = w_gate.shape[2]
    wspec = pl.BlockSpec((None, k, tn), lambda i, j: (layer, 0, j))
    return pl.pallas_call(
        _swiglu_kernel,
        grid=(m // tm, n // tn),
        in_specs=[pl.BlockSpec((tm, k), lambda i, j: (i, 0)), wspec, wspec],
        out_specs=pl.BlockSpec((tm, tn), lambda i, j: (i, j)),
        out_shape=jax.ShapeDtypeStruct((m, n), BF16),
        compiler_params=_params("arbitrary", "arbitrary"),
        name="project_swiglu",
    )(a, w_gate, w_up)


def _rows(start, dil):
    return pl.ds(start, Q_BLOCK) if dil == 1 else pl.ds(start, Q_BLOCK, stride=dil)


def _swa_prompt_kernel(sl_ref, q0, k0, v0, q1, k1, v1, q2, k2, v2, o_ref,
                       og_scr, lse_scr, bias_scr, *, seq):
    qkv = ((q0, k0, v0), (q1, k1, v1), (q2, k2, v2))
    scale = HEAD_DIM ** -0.5
    qi = lax.broadcasted_iota(jnp.int32, (Q_BLOCK, Q_BLOCK), 0)
    kj = lax.broadcasted_iota(jnp.int32, (Q_BLOCK, Q_BLOCK), 1)

    for g, dil in enumerate(SWA_DILATIONS):
        slope = sl_ref[g:g + 1, :]
        d_cur = qi - kj
        d_prev = Q_BLOCK + qi - kj
        bias_scr[g, 0] = jnp.where(d_cur >= 0, -(slope * (d_cur * dil).astype(F32)), MASK_VALUE)
        bias_scr[g, 1] = jnp.where(d_prev <= SWA_SPAN, -(slope * (d_prev * dil).astype(F32)), MASK_VALUE)

    for g, dil in enumerate(SWA_DILATIONS):
        q_ref, k_ref, v_ref = qkv[g]
        blocks_per_res = seq // (dil * Q_BLOCK)

        def block(start, with_prev, g=g, dil=dil, q_ref=q_ref, k_ref=k_ref, v_ref=v_ref):
            rows = _rows(start, dil)
            q = q_ref[rows, :].astype(BF16)
            kc = k_ref[rows, :].astype(BF16)
            vc = v_ref[rows, :].astype(BF16)
            s_c = _dot(q, kc, NT_DIMS) * scale + bias_scr[g, 0]
            m = jnp.max(s_c, axis=-1, keepdims=True)
            if with_prev:
                prow = _rows(start - Q_BLOCK * dil, dil)
                kp = k_ref[prow, :].astype(BF16)
                vp = v_ref[prow, :].astype(BF16)
                s_p = _dot(q, kp, NT_DIMS) * scale + bias_scr[g, 1]
                m = jnp.maximum(m, jnp.max(s_p, axis=-1, keepdims=True))
            p_c = jnp.exp(s_c - m)
            den = jnp.sum(p_c, axis=-1, keepdims=True)
            if with_prev:
                p_p = jnp.exp(s_p - m)
                den = den + jnp.sum(p_p, axis=-1, keepdims=True)
            inv = 1.0 / den
            o = _dot((p_c * inv).astype(BF16), vc)
            if with_prev:
                o = o + _dot((p_p * inv).astype(BF16), vp)
            og_scr[g, rows, :] = o
            lse_scr[g, rows, :] = jnp.broadcast_to(m + jnp.log(den), (Q_BLOCK, LANES))

        def first_body(r, carry, block=block):
            block(r, False)
            return carry

        lax.fori_loop(0, dil, first_body, 0)

        if blocks_per_res > 1:
            per = blocks_per_res - 1

            def rest_body(it, carry, block=block, per=per, dil=dil):
                r = it // per
                mb = it % per + 1
                block(r + mb * (Q_BLOCK * dil), True)
                return carry

            lax.fori_loop(0, dil * per, rest_body, 0)

    chunk = 256

    def merge_body(c, carry):
        rows = pl.ds(pl.multiple_of(c * chunk, chunk), chunk)
        l0, l1, l2 = lse_scr[0, rows, :], lse_scr[1, rows, :], lse_scr[2, rows, :]
        mx = jnp.maximum(jnp.maximum(l0, l1), l2)
        w0, w1, w2 = jnp.exp(l0 - mx), jnp.exp(l1 - mx), jnp.exp(l2 - mx)
        inv = 1.0 / (w0 + w1 + w2)
        o = (w0 * inv) * og_scr[0, rows, :] + (w1 * inv) * og_scr[1, rows, :] + (w2 * inv) * og_scr[2, rows, :]
        o_ref[rows, :] = o.astype(o_ref.dtype)
        return carry

    lax.fori_loop(0, seq // chunk, merge_body, 0)


def swa_prompt_attention(qkv, slopes_tab, batch, seq):
    _, m, d = qkv.shape
    heads = d // HEAD_DIM
    specs = [pl.BlockSpec((None, SUBLANES, LANES), lambda b, h: (h, 0, 0))]
    for s in range(9):
        specs.append(pl.BlockSpec((None, seq, HEAD_DIM), lambda b, h, s=s: (s, b, h)))
    return pl.pallas_call(
        functools.partial(_swa_prompt_kernel, seq=seq),
        grid=(batch, heads),
        in_specs=specs,
        out_specs=pl.BlockSpec((seq, HEAD_DIM), lambda b, h: (b, h)),
        out_shape=jax.ShapeDtypeStruct((m, d), BF16),
        scratch_shapes=[
            pltpu.VMEM((3, seq, HEAD_DIM), F32),
            pltpu.VMEM((3, seq, LANES), F32),
            pltpu.VMEM((3, 2, Q_BLOCK, Q_BLOCK), F32),
        ],
        compiler_params=_params("arbitrary", "arbitrary"),
        name="swa_prompt_attention",
    )(slopes_tab, *([qkv] * 9))


def _swa_sample_kernel(sl_ref, *refs, t_new, past_len):
    qkv_new = refs[0:9]
    caches = refs[9:15]
    o_ref = refs[15]
    outs = refs[16:22]
    kall, vall = refs[22], refs[23]
    scale = HEAD_DIM ** -0.5
    pad = LANES
    m_run = l_run = acc = None
    for g, (win, dil) in enumerate(zip(SWA_WINDOWS, SWA_DILATIONS)):
        q = qkv_new[3 * g][...].astype(BF16)
        n_keys = win + pad
        for new_ref, cache_ref, out_ref, scr in (
                (qkv_new[3 * g + 1], caches[2 * g], outs[2 * g], kall),
                (qkv_new[3 * g + 2], caches[2 * g + 1], outs[2 * g + 1], vall)):
            scr[0:win, :] = cache_ref[...]
            scr[win:win + t_new, :] = new_ref[...]
            scr[win + t_new:n_keys, :] = jnp.zeros((pad - t_new, HEAD_DIM), F32)
            out_ref[...] = scr[t_new:win + t_new, :]
        k = kall[0:n_keys, :].astype(BF16)
        v = vall[0:n_keys, :].astype(BF16)
        slope = sl_ref[g:g + 1, 0:1]
        t = lax.broadcasted_iota(jnp.int32, (t_new, n_keys), 0)
        idx = lax.broadcasted_iota(jnp.int32, (t_new, n_keys), 1)
        dist = win + t - idx
        valid = (dist >= 0) & ((dist & (dil - 1)) == 0) & (dist <= SWA_SPAN * dil)
        valid = valid & ((past_len - win) + idx >= 0)
        s = _dot(q, k, NT_DIMS) * scale - slope * dist.astype(F32)
        s = jnp.where(valid, s, MASK_VALUE)
        m = jnp.max(s, axis=-1, keepdims=True)
        p = jnp.exp(s - m)
        den = jnp.sum(p, axis=-1, keepdims=True)
        o = _dot((p / den).astype(BF16), v)
        lse = m + jnp.log(den)
        if g == 0:
            m_run, l_run, acc = lse, jnp.ones_like(lse), o
        else:
            m_new = jnp.maximum(m_run, lse)
            a_old, a_new = jnp.exp(m_run - m_new), jnp.exp(lse - m_new)
            l_run = l_run * a_old + a_new
            acc = acc * a_old + o * a_new
            m_run = m_new
    o_ref[...] = acc / l_run


def swa_sample_attention(qkv_new, caches, layer, slopes_tab, dec_batch, t_new, past_len):
    _, m, d = qkv_new.shape
    heads = d // HEAD_DIM
    specs = [pl.BlockSpec((None, SUBLANES, LANES), lambda b, h: (h, 0, 0))]
    for s in range(9):
        specs.append(pl.BlockSpec((None, t_new, HEAD_DIM), lambda b, h, s=s: (s, b, h)))
    out_specs = [pl.BlockSpec((t_new, HEAD_DIM), lambda b, h: (b, h))]
    out_shapes = [jax.ShapeDtypeStruct((m, d), F32)]
    for g, win in enumerate(SWA_WINDOWS):
        for _ in range(2):
            specs.append(pl.BlockSpec((None, None, win, HEAD_DIM), lambda b, h: (layer, b, 0, h)))
            out_specs.append(pl.BlockSpec((None, win, HEAD_DIM), lambda b, h: (b, 0, h)))
            out_shapes.append(jax.ShapeDtypeStruct((dec_batch, win, d), F32))
    max_keys = max(SWA_WINDOWS) + LANES
    return pl.pallas_call(
        functools.partial(_swa_sample_kernel, t_new=t_new, past_len=past_len),
        grid=(dec_batch, heads),
        in_specs=specs,
        out_specs=out_specs,
        out_shape=out_shapes,
        scratch_shapes=[pltpu.VMEM((max_keys, HEAD_DIM), F32), pltpu.VMEM((max_keys, HEAD_DIM), F32)],
        compiler_params=_params("arbitrary", "arbitrary"),
        name="swa_sample_attention",
    )(slopes_tab, *([qkv_new] * 9), *caches)


def _sb_prompt_kernel(b_ref, q_ref, k_ref, v_ref, o_ref, *, tq):
    qb = pl.program_id(2)
    q = q_ref[...].astype(BF16)
    bias = b_ref[0:1, :]
    row = lax.broadcasted_iota(jnp.int32, (tq, tq), 0)
    col = lax.broadcasted_iota(jnp.int32, (tq, tq), 1)
    newer_mat = jnp.where(row > col, 1.0, 0.0).astype(BF16)
    causal = col < row

    def key_block(kb, carry, acc, masked):
        ks = pl.ds(pl.multiple_of(kb * tq, tq), tq)
        k = k_ref[ks, :].astype(BF16)
        v = v_ref[ks, :].astype(BF16)
        z = _dot(q, k, NT_DIMS) + bias
        log_keep = -_softplus(z)
        log_beta = z + log_keep
        if masked:
            log_keep = jnp.where(causal, log_keep, 0.0)
        hi, lo = _split_bf16(log_keep)
        newer = _dot(hi, newer_mat) + _dot(lo, newer_mat) + carry
        a = jnp.exp(log_beta + newer)
        if masked:
            a = jnp.where(causal, a, 0.0)
        acc = acc + _dot(a.astype(BF16), v)
        carry = carry + jnp.sum(log_keep, axis=-1, keepdims=True)
        return carry, acc

    carry0 = jnp.zeros((tq, 1), F32)
    acc0 = jnp.zeros((tq, HEAD_DIM), F32)
    carry, acc = key_block(qb, carry0, acc0, True)

    def body(it, state):
        return key_block(qb - 1 - it, state[0], state[1], False)

    carry, acc = lax.fori_loop(0, qb, body, (carry, acc))
    o_ref[...] = acc.astype(o_ref.dtype)


def sb_prompt_attention(qkv, bias_tab, batch, seq, tq):
    _, m, d = qkv.shape
    heads = d // HEAD_DIM
    nq = seq // tq
    return pl.pallas_call(
        functools.partial(_sb_prompt_kernel, tq=tq),
        grid=(batch, heads, nq),
        in_specs=[
            pl.BlockSpec((None, SUBLANES, tq), lambda b, h, i: (h, 0, 0)),
            pl.BlockSpec((None, tq, HEAD_DIM), lambda b, h, i: (0, b * nq + i, h)),
            pl.BlockSpec((None, seq, HEAD_DIM), lambda b, h, i: (1, b, h)),
            pl.BlockSpec((None, seq, HEAD_DIM), lambda b, h, i: (2, b, h)),
        ],
        out_specs=pl.BlockSpec((tq, HEAD_DIM), lambda b, h, i: (b * nq + i, h)),
        out_shape=jax.ShapeDtypeStruct((m, d), BF16),
        compiler_params=_params("arbitrary", "arbitrary", "arbitrary"),
        name="sb_prompt_attention",
    )(bias_tab, qkv, qkv, qkv)


def _sb_sample_kernel(pt_ref, qbd_ref, bias_ref, kn_ref, vn_ref, *refs, pages_per_step, t_new, heads):
    del pt_ref
    k_refs = refs[:pages_per_step]
    v_refs = refs[pages_per_step:2 * pages_per_step]
    o_ref = refs[2 * pages_per_step]
    acc_scr, carry_scr = refs[2 * pages_per_step + 1:]
    step = pl.program_id(1)
    qbd = qbd_ref[...].astype(BF16)
    bias = bias_ref[...]
    n_col = heads * t_new
    key = lax.broadcasted_iota(jnp.int32, (PAGE_SIZE, n_col), 0)
    tok = lax.broadcasted_iota(jnp.int32, (PAGE_SIZE, n_col), 1) % t_new
    row = lax.broadcasted_iota(jnp.int32, (PAGE_SIZE, PAGE_SIZE), 0)
    col = lax.broadcasted_iota(jnp.int32, (PAGE_SIZE, PAGE_SIZE), 1)
    newer_mat = jnp.where(col > row, 1.0, 0.0).astype(BF16)

    def page(k, v, masked):
        z = _dot(k.astype(BF16), qbd) + bias
        log_keep = -_softplus(z)
        log_beta = z + log_keep
        if masked:
            causal = key < tok
            log_keep = jnp.where(causal, log_keep, 0.0)
        hi, lo = _split_bf16(log_keep)
        newer = _dot(newer_mat, hi) + _dot(newer_mat, lo) + carry_scr[...]
        a = jnp.exp(log_beta + newer)
        if masked:
            a = jnp.where(causal, a, 0.0)
        acc_scr[...] += _dot(a.astype(BF16), v.astype(BF16), TN_DIMS)
        carry_scr[...] += jnp.sum(log_keep, axis=0, keepdims=True)

    @pl.when(step == 0)
    def _():
        acc_scr[...] = jnp.zeros_like(acc_scr)
        carry_scr[...] = jnp.zeros_like(carry_scr)
        page(kn_ref[...], vn_ref[...], True)

    for p in range(pages_per_step):
        page(k_refs[p][...], v_refs[p][...], False)

    @pl.when(step == pl.num_programs(1) - 1)
    def _():
        for h in range(heads):
            o_ref[:, h * HEAD_DIM:(h + 1) * HEAD_DIM] = acc_scr[h * t_new:(h + 1) * t_new,
                                                                h * HEAD_DIM:(h + 1) * HEAD_DIM]


def sb_sample_attention(q_new, k_new, v_new, cache_k, cache_v, layer, page_table, bias, t_new, pages_per_step=4):
    m, d = q_new.shape
    heads = d // HEAD_DIM
    dec_batch, n_pages = page_table.shape
    n_col = heads * t_new
    q4 = q_new.reshape(dec_batch, t_new, heads, HEAD_DIM)
    qbd = jnp.einsum('bthe,hg->bhegt', q4, jnp.eye(heads, dtype=F32)).reshape(dec_batch, d, n_col)
    bias_cols = jnp.repeat(bias.astype(F32), t_new).reshape(1, n_col)
    pad = ((0, 0), (0, PAGE_SIZE - t_new), (0, 0))
    kn = jnp.pad(k_new.reshape(dec_batch, t_new, d), pad)
    vn = jnp.pad(v_new.reshape(dec_batch, t_new, d), pad)
    steps = n_pages // pages_per_step

    def page_spec(p):
        return pl.BlockSpec(
            (None, None, PAGE_SIZE, d),
            lambda b, s, pt, p=p: (layer, pt[b, n_pages - 1 - (s * pages_per_step + p)], 0, 0))

    grid_spec = pltpu.PrefetchScalarGridSpec(
        num_scalar_prefetch=1,
        grid=(dec_batch, steps),
        in_specs=[
            pl.BlockSpec((None, d, n_col), lambda b, s, pt: (b, 0, 0)),
            pl.BlockSpec((1, n_col), lambda b, s, pt: (0, 0)),
            pl.BlockSpec((None, PAGE_SIZE, d), lambda b, s, pt: (b, 0, 0)),
            pl.BlockSpec((None, PAGE_SIZE, d), lambda b, s, pt: (b, 0, 0)),
        ] + [page_spec(p) for p in range(pages_per_step)] * 2,
        out_specs=pl.BlockSpec((t_new, d), lambda b, s, pt: (b, 0)),
        scratch_shapes=[pltpu.VMEM((n_col, d), F32), pltpu.VMEM((1, n_col), F32)],
    )
    return pl.pallas_call(
        functools.partial(_sb_sample_kernel, pages_per_step=pages_per_step, t_new=t_new, heads=heads),
        grid_spec=grid_spec,
        out_shape=jax.ShapeDtypeStruct((m, d), F32),
        compiler_params=_params("arbitrary", "arbitrary"),
        name="sb_sample_attention",
    )(page_table, qbd, bias_cols, kn, vn, *([cache_k] * pages_per_step), *([cache_v] * pages_per_step))


def _alibi_slopes(heads):
    n = len(SWA_DILATIONS) * heads
    s = 2.0 ** (-8.0 * np.arange(1, n + 1) / n)
    return jnp.asarray(s.reshape(len(SWA_DILATIONS), heads), dtype=F32)


def _lane_table(per_head, width):
    heads, n = per_head.shape
    tab = jnp.pad(per_head, ((0, 0), (0, SUBLANES - n)))
    return jnp.broadcast_to(tab[:, :, None], (heads, SUBLANES, width))


def _last_rows(t, n):
    seq = t.shape[1]
    if seq >= n:
        return t[:, seq - n:]
    return jnp.pad(t, ((0, 0), (n - seq, 0), (0, 0), (0, 0)))


def kernel(x_prompt, x_sample, cache_swa_k0, cache_swa_v0, cache_swa_k1, cache_swa_v1, cache_swa_k2, cache_swa_v2, cache_sb_k, cache_sb_v, page_table, c_prompt, c_sample, w_ada, b_ada, norm_mix_g, norm_ffn_g, w_qkv_swa, w_o_swa, w_qkv_sb, w_o_sb, sb_bias, w_gate, w_up, w_down, final_norm_g):
    batch, seq, d = x_prompt.shape
    dec_batch, t_new, _ = x_sample.shape
    depth = w_ada.shape[0]
    heads = d // HEAD_DIM
    n_pages = page_table.shape[1]
    past_len = n_pages * cache_sb_k.shape[2]
    mp_rows, ms_rows = batch * seq, dec_batch * t_new
    tm_p, tm_s = 1024, ms_rows
    sb_tq = 256

    slopes_tab = _lane_table(_alibi_slopes(heads).T, LANES)
    swa_caches = [c.reshape(c.shape[0], dec_batch, c.shape[2], d) for c in
                  (cache_swa_k0, cache_swa_v0, cache_swa_k1, cache_swa_v1, cache_swa_k2, cache_swa_v2)]
    sb_k_pool = cache_sb_k.reshape(cache_sb_k.shape[0], cache_sb_k.shape[1], PAGE_SIZE, d)
    sb_v_pool = cache_sb_v.reshape(cache_sb_v.shape[0], cache_sb_v.shape[1], PAGE_SIZE, d)

    n_c = batch + dec_batch
    c_rows = -(-n_c // SUBLANES) * SUBLANES
    c_all = jnp.pad(jnp.concatenate([c_prompt, c_sample], axis=0), ((0, c_rows - n_c), (0, 0)))
    mods = ada_all(c_all, w_ada, b_ada)

    xp = x_prompt.reshape(mp_rows, d)
    xs = x_sample.reshape(ms_rows, d)
    swa_p = [[] for _ in range(6)]
    swa_s = [[] for _ in range(6)]
    sb_p = [[], []]
    sb_s = [[], []]

    for i in range(depth):
        mod_i = mods[i].reshape(c_rows, 6, d)
        mp = [mod_i[:batch, j][:, None, :] for j in range(6)]
        ms = [jnp.repeat(mod_i[batch:n_c, j], t_new, axis=0)[None] for j in range(6)]
        hp = norm_mod(xp, norm_mix_g[i], mp[0], mp[1], tm_p)
        hs = norm_mod(xs, norm_mix_g[i], ms[0], ms[1], tm_s)
        if i % 2 == 0:
            a = i // 2
            qkv_p = project_segments(hp, w_qkv_swa, a, d, tm_p, 1024)
            qkv_s = project_segments(hs, w_qkv_swa, a, d, tm_s, 1024)
            op = swa_prompt_attention(qkv_p, slopes_tab, batch, seq)
            res = swa_sample_attention(qkv_s, swa_caches, a, slopes_tab, dec_batch, t_new, past_len)
            os_ = res[0]
            for g, win in enumerate(SWA_WINDOWS):
                for c in range(2):
                    full = qkv_p[3 * g + 1 + c].reshape(batch, seq, heads, HEAD_DIM)
                    swa_p[2 * g + c].append(_last_rows(full, win))
                    swa_s[2 * g + c].append(res[1 + 2 * g + c].reshape(dec_batch, win, heads, HEAD_DIM))
            w_o, la = w_o_swa, a
        else:
            b = i // 2
            qkv_p = project_segments(hp, w_qkv_sb, b, d, tm_p, 1024, scaled_seg0=HEAD_DIM ** -0.5)
            qkv_s = project_segments(hs, w_qkv_sb, b, d, tm_s, 1024, scaled_seg0=HEAD_DIM ** -0.5)
            bias_tab = _lane_table(sb_bias[b][:, None], sb_tq)
            op = sb_prompt_attention(qkv_p, bias_tab, batch, seq, sb_tq)
            os_ = sb_sample_attention(qkv_s[0], qkv_s[1], qkv_s[2], sb_k_pool, sb_v_pool, b,
                                      page_table, sb_bias[b], t_new)
            for c in range(2):
                sb_p[c].append(qkv_p[1 + c].reshape(batch, seq, heads, HEAD_DIM))
                sb_s[c].append(qkv_s[1 + c].reshape(dec_batch, t_new, heads, HEAD_DIM))
            w_o, la = w_o_sb, b
        xp = project_residual(op, w_o, la, xp, mp[2], tm_p, 1024)
        xs = project_residual(os_, w_o, la, xs, ms[2], tm_s, 1024)
        hp = norm_mod(xp, norm_ffn_g[i], mp[3], mp[4], tm_p)
        hs = norm_mod(xs, norm_ffn_g[i], ms[3], ms[4], tm_s)
        ap = project_swiglu(hp, w_gate, w_up, i, tm_p, 512)
        as_ = project_swiglu(hs, w_gate, w_up, i, tm_s, 512)
        xp = project_residual(ap, w_down, i, xp, mp[5], tm_p, 256)
        xs = project_residual(as_, w_down, i, xs, ms[5], tm_s, 256)

    y_prompt = final_norm(xp, final_norm_g, tm_p).reshape(batch, seq, d)
    y_sample = final_norm(xs, final_norm_g, tm_s).reshape(dec_batch, t_new, d)
    outs = [y_prompt, y_sample]
    outs += [jnp.stack(t) for t in swa_p]
    outs += [jnp.stack(t) for t in sb_p]
    outs += [jnp.stack(t) for t in swa_s]
    outs += [jnp.stack(t) for t in sb_s]
    return tuple(outs)
```

```python
import functools
import math

import numpy as np
import jax
import jax.numpy as jnp
from jax import lax
from jax.experimental import pallas as pl
from jax.experimental.pallas import tpu as pltpu

F32 = jnp.float32
BF16 = jnp.bfloat16

HEAD_DIM = 128
SWA_WINDOWS = (128, 512, 2048)
SWA_DILATIONS = (1, 4, 16)
SWA_SPAN = 128
Q_BLOCK = 128
PAGE_SIZE = 128
EPS = 1e-6
MASK_VALUE = -1e30
LANES = 128
SUBLANES = 8
VMEM_LIMIT_BYTES = 56 * 1024 * 1024

NT_DIMS = (((1,), (1,)), ((), ()))
TN_DIMS = (((0,), (0,)), ((), ()))


def _params(*sem):
    return pltpu.CompilerParams(dimension_semantics=sem, vmem_limit_bytes=VMEM_LIMIT_BYTES)


def _dot(a, b, dims=None):
    if dims is None:
        return jnp.dot(a, b, preferred_element_type=F32)
    return lax.dot_general(a, b, dims, preferred_element_type=F32)


def _softplus(z):
    return jnp.maximum(z, 0.0) + jnp.log(1.0 + jnp.exp(-jnp.abs(z)))


def _split_bf16(x):
    hi = x.astype(BF16)
    lo = (x - hi.astype(F32)).astype(BF16)
    return hi, lo


def _ada_kernel(c_ref, w_ref, b_ref, o_ref):
    c = c_ref[...]
    a = (c / (1.0 + jnp.exp(-c))).astype(BF16)
    o_ref[...] = _dot(a, w_ref[...].astype(BF16)) + b_ref[...]


def ada_all(c_all, w_ada, b_ada, tn=1024):
    depth, d, n = w_ada.shape
    r = c_all.shape[0]
    return pl.pallas_call(
        _ada_kernel,
        grid=(depth, n // tn),
        in_specs=[
            pl.BlockSpec((r, d), lambda l, j: (0, 0)),
            pl.BlockSpec((None, d, tn), lambda l, j: (l, 0, j)),
            pl.BlockSpec((None, 1, tn), lambda l, j: (l, 0, j)),
        ],
        out_specs=pl.BlockSpec((None, r, tn), lambda l, j: (l, 0, j)),
        out_shape=jax.ShapeDtypeStruct((depth, r, n), F32),
        compiler_params=_params("arbitrary", "arbitrary"),
        name="ada_mod",
    )(c_all, w_ada, b_ada.reshape(depth, 1, n))


def _rms(x):
    return x * lax.rsqrt(jnp.mean(x * x, axis=-1, keepdims=True) + EPS)


def _norm_mod_kernel(x_ref, g_ref, sh_ref, sc_ref, o_ref):
    y = _rms(x_ref[...]) * g_ref[...]
    o_ref[...] = (y * (1.0 + sc_ref[...]) + sh_ref[...]).astype(o_ref.dtype)


def _norm_kernel(x_ref, g_ref, o_ref):
    o_ref[...] = (_rms(x_ref[...]) * g_ref[...]).astype(o_ref.dtype)


def _mod_spec(mod, tiles_per_group, width, two_d):
    r = mod.shape[1]
    if two_d:
        return pl.BlockSpec((None, r, width), lambda i, j: (i // tiles_per_group, 0, j))
    return pl.BlockSpec((None, r, width), lambda i: (i // tiles_per_group, 0, 0))


def norm_mod(x, g, shift, scale, tm):
    m, d = x.shape
    tpg = (m // tm) // shift.shape[0]
    return pl.pallas_call(
        _norm_mod_kernel,
        grid=(m // tm,),
        in_specs=[
            pl.BlockSpec((tm, d), lambda i: (i, 0)),
            pl.BlockSpec((1, d), lambda i: (0, 0)),
            _mod_spec(shift, tpg, d, False),
            _mod_spec(scale, tpg, d, False),
        ],
        out_specs=pl.BlockSpec((tm, d), lambda i: (i, 0)),
        out_shape=jax.ShapeDtypeStruct((m, d), BF16),
        compiler_params=_params("arbitrary"),
        name="norm_mod",
    )(x, g.reshape(1, d), shift, scale)


def final_norm(x, g, tm):
    m, d = x.shape
    return pl.pallas_call(
        _norm_kernel,
        grid=(m // tm,),
        in_specs=[pl.BlockSpec((tm, d), lambda i: (i, 0)), pl.BlockSpec((1, d), lambda i: (0, 0))],
        out_specs=pl.BlockSpec((tm, d), lambda i: (i, 0)),
        out_shape=jax.ShapeDtypeStruct((m, d), F32),
        compiler_params=_params("arbitrary"),
        name="final_norm",
    )(x, g.reshape(1, d))


def _proj_kernel(a_ref, w_ref, o_ref, *, scaled_tiles, scale):
    acc = _dot(a_ref[...].astype(BF16), w_ref[...].astype(BF16))
    if scaled_tiles:
        acc = acc * jnp.where(pl.program_id(1) < scaled_tiles, scale, 1.0).astype(F32)
    o_ref[...] = acc


def project_segments(a, w, layer, seg, tm, tn, scaled_seg0=None):
    m, k = a.shape
    n = w.shape[2]
    tps = seg // tn
    kern = functools.partial(
        _proj_kernel,
        scaled_tiles=tps if scaled_seg0 is not None else 0,
        scale=scaled_seg0 if scaled_seg0 is not None else 1.0,
    )
    return pl.pallas_call(
        kern,
        grid=(m // tm, n // tn),
        in_specs=[
            pl.BlockSpec((tm, k), lambda i, j: (i, 0)),
            pl.BlockSpec((None, k, tn), lambda i, j: (layer, 0, j)),
        ],
        out_specs=pl.BlockSpec((None, tm, tn), lambda i, j: (j // tps, i, j % tps)),
        out_shape=jax.ShapeDtypeStruct((n // seg, m, seg), F32),
        compiler_params=_params("arbitrary", "arbitrary"),
        name="project_segments",
    )(a, w)


def _residual_kernel(a_ref, w_ref, x_ref, g_ref, o_ref):
    acc = _dot(a_ref[...].astype(BF16), w_ref[...].astype(BF16))
    o_ref[...] = x_ref[...] + g_ref[...] * acc


def project_residual(a, w, layer, x, gate, tm, tn):
    m, k = a.shape
    n = w.shape[2]
    tpg = (m // tm) // gate.shape[0]
    return pl.pallas_call(
        _residual_kernel,
        grid=(m // tm, n // tn),
        in_specs=[
            pl.BlockSpec((tm, k), lambda i, j: (i, 0)),
            pl.BlockSpec((None, k, tn), lambda i, j: (layer, 0, j)),
            pl.BlockSpec((tm, tn), lambda i, j: (i, j)),
            _mod_spec(gate, tpg, tn, True),
        ],
        out_specs=pl.BlockSpec((tm, tn), lambda i, j: (i, j)),
        out_shape=jax.ShapeDtypeStruct((m, n), F32),
        compiler_params=_params("arbitrary", "arbitrary"),
        name="project_residual",
    )(a, w, x, gate)


def _swiglu_kernel(a_ref, wg_ref, wu_ref, o_ref):
    a = a_ref[...].astype(BF16)
    g = _dot(a, wg_ref[...].astype(BF16))
    u = _dot(a, wu_ref[...].astype(BF16))
    o_ref[...] = ((g / (1.0 + jnp.exp(-g))) * u).astype(o_ref.dtype)


def project_swiglu(a, w_gate, w_up, layer, tm, tn):
    m, k = a.shape
    n = w_gate.shape[2]
    wspec = pl.BlockSpec((None, k, tn), lambda i, j: (layer, 0, j))
    return pl.pallas_call(
        _swiglu_kernel,
        grid=(m // tm, n // tn),
        in_specs=[pl.BlockSpec((tm, k), lambda i, j: (i, 0)), wspec, wspec],
        out_specs=pl.BlockSpec((tm, tn), lambda i, j: (i, j)),
        out_shape=jax.ShapeDtypeStruct((m, n), BF16),
        compiler_params=_params("arbitrary", "arbitrary"),
        name="project_swiglu",
    )(a, w_gate, w_up)


def _rows(start, dil):
    return pl.ds(start, Q_BLOCK) if dil == 1 else pl.ds(start, Q_BLOCK, stride=dil)


def _pick_unroll(n):
    for u in (4, 5, 3, 2):
        if n % u == 0:
            return u
    return 1


def _swa_prompt_kernel(sl_ref, q0, k0, v0, q1, k1, v1, q2, k2, v2, o_ref,
                       og_scr, lse_scr, bias_scr, *, seq):
    qkv = ((q0, k0, v0), (q1, k1, v1), (q2, k2, v2))
    scale = HEAD_DIM ** -0.5
    qi = lax.broadcasted_iota(jnp.int32, (Q_BLOCK, Q_BLOCK), 0)
    kj = lax.broadcasted_iota(jnp.int32, (Q_BLOCK, Q_BLOCK), 1)

    for g, dil in enumerate(SWA_DILATIONS):
        slope = sl_ref[g:g + 1, :]
        d_cur = qi - kj
        d_prev = Q_BLOCK + qi - kj
        bias_scr[g, 0] = jnp.where(d_cur >= 0, -(slope * (d_cur * dil).astype(F32)), MASK_VALUE)
        bias_scr[g, 1] = jnp.where(d_prev <= SWA_SPAN, -(slope * (d_prev * dil).astype(F32)), MASK_VALUE)

    for g, dil in enumerate(SWA_DILATIONS):
        q_ref, k_ref, v_ref = qkv[g]
        blocks_per_res = seq // (dil * Q_BLOCK)

        def blocks(starts, with_prev, g=g, dil=dil, q_ref=q_ref, k_ref=k_ref, v_ref=v_ref):
            n = len(starts)
            rows = [_rows(s, dil) for s in starts]
            q = [q_ref[r, :].astype(BF16) for r in rows]
            kc = [k_ref[r, :].astype(BF16) for r in rows]
            vc = [v_ref[r, :].astype(BF16) for r in rows]
            if with_prev:
                prow = [_rows(s - Q_BLOCK * dil, dil) for s in starts]
                kp = [k_ref[r, :].astype(BF16) for r in prow]
                vp = [v_ref[r, :].astype(BF16) for r in prow]
            s_c = [_dot(q[i], kc[i], NT_DIMS) * scale + bias_scr[g, 0] for i in range(n)]
            m = [jnp.max(s, axis=-1, keepdims=True) for s in s_c]
            if with_prev:
                s_p = [_dot(q[i], kp[i], NT_DIMS) * scale + bias_scr[g, 1] for i in range(n)]
                m = [jnp.maximum(m[i], jnp.max(s_p[i], axis=-1, keepdims=True)) for i in range(n)]
            p_c = [jnp.exp(s_c[i] - m[i]) for i in range(n)]
            den = [jnp.sum(p, axis=-1, keepdims=True) for p in p_c]
            if with_prev:
                p_p = [jnp.exp(s_p[i] - m[i]) for i in range(n)]
                den = [den[i] + jnp.sum(p_p[i], axis=-1, keepdims=True) for i in range(n)]
            inv = [1.0 / d_ for d_ in den]
            o = [_dot((p_c[i] * inv[i]).astype(BF16), vc[i]) for i in range(n)]
            if with_prev:
                o = [o[i] + _dot((p_p[i] * inv[i]).astype(BF16), vp[i]) for i in range(n)]
            lse = [jnp.broadcast_to(m[i] + jnp.log(den[i]), (Q_BLOCK, LANES)) for i in range(n)]
            for i in range(n):
                og_scr[g, rows[i], :] = o[i]
                lse_scr[g, rows[i], :] = lse[i]

        def run(n_blocks, start_of, with_prev, blocks=blocks):
            unroll = _pick_unroll(n_blocks)

            def body(it, carry):
                blocks([start_of(it * unroll + u) for u in range(unroll)], with_prev)
                return carry

            lax.fori_loop(0, n_blocks // unroll, body, 0)

        run(dil, lambda r: r, False)
        if blocks_per_res > 1:
            per = blocks_per_res - 1
            run(dil * per, lambda it, per=per, dil=dil: it // per + (it % per + 1) * (Q_BLOCK * dil), True)

    chunk = 256

    def merge_body(c, carry):
        rows = pl.ds(pl.multiple_of(c * chunk, chunk), chunk)
        l0, l1, l2 = lse_scr[0, rows, :], lse_scr[1, rows, :], lse_scr[2, rows, :]
        mx = jnp.maximum(jnp.maximum(l0, l1), l2)
        w0, w1, w2 = jnp.exp(l0 - mx), jnp.exp(l1 - mx), jnp.exp(l2 - mx)
        inv = 1.0 / (w0 + w1 + w2)
        o = (w0 * inv) * og_scr[0, rows, :] + (w1 * inv) * og_scr[1, rows, :] + (w2 * inv) * og_scr[2, rows, :]
        o_ref[rows, :] = o.astype(o_ref.dtype)
        return carry

    lax.fori_loop(0, seq // chunk, merge_body, 0)


def swa_prompt_attention(qkv, slopes_tab, batch, seq):
    _, m, d = qkv.shape
    heads = d // HEAD_DIM
    specs = [pl.BlockSpec((None, SUBLANES, LANES), lambda b, h: (h, 0, 0))]
    for s in range(9):
        specs.append(pl.BlockSpec((None, seq, HEAD_DIM), lambda b, h, s=s: (s, b, h)))
    return pl.pallas_call(
        functools.partial(_swa_prompt_kernel, seq=seq),
        grid=(batch, heads),
        in_specs=specs,
        out_specs=pl.BlockSpec((seq, HEAD_DIM), lambda b, h: (b, h)),
        out_shape=jax.ShapeDtypeStruct((m, d), BF16),
        scratch_shapes=[
            pltpu.VMEM((3, seq, HEAD_DIM), F32),
            pltpu.VMEM((3, seq, LANES), F32),
            pltpu.VMEM((3, 2, Q_BLOCK, Q_BLOCK), F32),
        ],
        compiler_params=_params("arbitrary", "arbitrary"),
        name="swa_prompt_attention",
    )(slopes_tab, *([qkv] * 9))


def _swa_sample_kernel(sl_ref, *refs, t_new, past_len):
    qkv_new = refs[0:9]
    caches = refs[9:15]
    o_ref = refs[15]
    outs = refs[16:22]
    kall, vall = refs[22], refs[23]
    scale = HEAD_DIM ** -0.5
    pad = LANES
    m_run = l_run = acc = None
    for g, (win, dil) in enumerate(zip(SWA_WINDOWS, SWA_DILATIONS)):
        q = qkv_new[3 * g][...].astype(BF16)
        n_keys = win + pad
        for new_ref, cache_ref, out_ref, scr in (
                (qkv_new[3 * g + 1], caches[2 * g], outs[2 * g], kall),
                (qkv_new[3 * g + 2], caches[2 * g + 1], outs[2 * g + 1], vall)):
            scr[0:win, :] = cache_ref[...]
            scr[win:win + t_new, :] = new_ref[...]
            scr[win + t_new:n_keys, :] = jnp.zeros((pad - t_new, HEAD_DIM), F32)
            out_ref[...] = scr[t_new:win + t_new, :]
        k = kall[0:n_keys, :].astype(BF16)
        v = vall[0:n_keys, :].astype(BF16)
        slope = sl_ref[g:g + 1, 0:1]
        t = lax.broadcasted_iota(jnp.int32, (t_new, n_keys), 0)
        idx = lax.broadcasted_iota(jnp.int32, (t_new, n_keys), 1)
        dist = win + t - idx
        valid = (dist >= 0) & ((dist & (dil - 1)) == 0) & (dist <= SWA_SPAN * dil)
        valid = valid & ((past_len - win) + idx >= 0)
        s = _dot(q, k, NT_DIMS) * scale - slope * dist.astype(F32)
        s = jnp.where(valid, s, MASK_VALUE)
        m = jnp.max(s, axis=-1, keepdims=True)
        p = jnp.exp(s - m)
        den = jnp.sum(p, axis=-1, keepdims=True)
        o = _dot((p / den).astype(BF16), v)
        lse = m + jnp.log(den)
        if g == 0:
            m_run, l_run, acc = lse, jnp.ones_like(lse), o
        else:
            m_new = jnp.maximum(m_run, lse)
            a_old, a_new = jnp.exp(m_run - m_new), jnp.exp(lse - m_new)
            l_run = l_run * a_old + a_new
            acc = acc * a_old + o * a_new
            m_run = m_new
    o_ref[...] = acc / l_run


def swa_sample_attention(qkv_new, caches, layer, slopes_tab, dec_batch, t_new, past_len):
    _, m, d = qkv_new.shape
    heads = d // HEAD_DIM
    specs = [pl.BlockSpec((None, SUBLANES, LANES), lambda b, h: (h, 0, 0))]
    for s in range(9):
        specs.append(pl.BlockSpec((None, t_new, HEAD_DIM), lambda b, h, s=s: (s, b, h)))
    out_specs = [pl.BlockSpec((t_new, HEAD_DIM), lambda b, h: (b, h))]
    out_shapes = [jax.ShapeDtypeStruct((m, d), F32)]
    for g, win in enumerate(SWA_WINDOWS):
        for _ in range(2):
            specs.append(pl.BlockSpec((None, None, win, HEAD_DIM), lambda b, h: (layer, b, 0, h)))
            out_specs.append(pl.BlockSpec((None, win, HEAD_DIM), lambda b, h: (b, 0, h)))
            out_shapes.append(jax.ShapeDtypeStruct((dec_batch, win, d), F32))
    max_keys = max(SWA_WINDOWS) + LANES
    return pl.pallas_call(
        functools.partial(_swa_sample_kernel, t_new=t_new, past_len=past_len),
        grid=(dec_batch, heads),
        in_specs=specs,
        out_specs=out_specs,
        out_shape=out_shapes,
        scratch_shapes=[pltpu.VMEM((max_keys, HEAD_DIM), F32), pltpu.VMEM((max_keys, HEAD_DIM), F32)],
        compiler_params=_params("arbitrary", "arbitrary"),
        name="swa_sample_attention",
    )(slopes_tab, *([qkv_new] * 9), *caches)


def _sb_prompt_kernel(b_ref, q_ref, k_ref, v_ref, o_ref, *, tq, hps):
    qb = pl.program_id(2)
    row = lax.broadcasted_iota(jnp.int32, (tq, tq), 0)
    col = lax.broadcasted_iota(jnp.int32, (tq, tq), 1)
    newer_mat = jnp.where(row > col, 1.0, 0.0).astype(BF16)
    causal = col < row

    def key_block(kb, carry, acc, masked):
        heads = range(hps)
        cols = [slice(h * HEAD_DIM, (h + 1) * HEAD_DIM) for h in heads]
        ks = pl.ds(pl.multiple_of(kb * tq, tq), tq)
        q = [q_ref[:, c].astype(BF16) for c in cols]
        k = [k_ref[ks, c].astype(BF16) for c in cols]
        v = [v_ref[ks, c].astype(BF16) for c in cols]
        z = [_dot(q[h], k[h], NT_DIMS) + b_ref[h:h + 1, :] for h in heads]
        log_keep = [-_softplus(z[h]) for h in heads]
        log_beta = [z[h] + log_keep[h] for h in heads]
        if masked:
            log_keep = [jnp.where(causal, lk, 0.0) for lk in log_keep]
        split = [_split_bf16(lk) for lk in log_keep]
        newer = [_dot(split[h][0], newer_mat) + _dot(split[h][1], newer_mat) + carry[h] for h in heads]
        a = [jnp.exp(log_beta[h] + newer[h]) for h in heads]
        if masked:
            a = [jnp.where(causal, x, 0.0) for x in a]
        acc = tuple(acc[h] + _dot(a[h].astype(BF16), v[h]) for h in heads)
        carry = tuple(carry[h] + jnp.sum(log_keep[h], axis=-1, keepdims=True) for h in heads)
        return carry, acc

    carry0 = tuple(jnp.zeros((tq, 1), F32) for _ in range(hps))
    acc0 = tuple(jnp.zeros((tq, HEAD_DIM), F32) for _ in range(hps))
    state = key_block(qb, carry0, acc0, True)
    state = lax.fori_loop(0, qb, lambda it, st: key_block(qb - 1 - it, st[0], st[1], False), state)
    for h in range(hps):
        o_ref[:, h * HEAD_DIM:(h + 1) * HEAD_DIM] = state[1][h].astype(o_ref.dtype)


def sb_prompt_attention(qkv, bias, batch, seq, tq, hps):
    _, m, d = qkv.shape
    heads = d // HEAD_DIM
    nq = seq // tq
    width = hps * HEAD_DIM
    bias_tab = _lane_table(bias.astype(F32).reshape(heads // hps, hps), tq)
    return pl.pallas_call(
        functools.partial(_sb_prompt_kernel, tq=tq, hps=hps),
        grid=(batch, heads // hps, nq),
        in_specs=[
            pl.BlockSpec((None, SUBLANES, tq), lambda b, h, i: (h, 0, 0)),
            pl.BlockSpec((None, tq, width), lambda b, h, i: (0, b * nq + i, h)),
            pl.BlockSpec((None, seq, width), lambda b, h, i: (1, b, h)),
            pl.BlockSpec((None, seq, width), lambda b, h, i: (2, b, h)),
        ],
        out_specs=pl.BlockSpec((tq, width), lambda b, h, i: (b * nq + i, h)),
        out_shape=jax.ShapeDtypeStruct((m, d), BF16),
        compiler_params=_params("arbitrary", "arbitrary", "arbitrary"),
        name="sb_prompt_attention",
    )(bias_tab, qkv, qkv, qkv)


def _sb_sample_kernel(pt_ref, qbd_ref, bias_ref, kn_ref, vn_ref, *refs, pages_per_step, t_new, heads):
    del pt_ref
    k_refs = refs[:pages_per_step]
    v_refs = refs[pages_per_step:2 * pages_per_step]
    o_ref = refs[2 * pages_per_step]
    carry_scr = refs[2 * pages_per_step + 1]
    step = pl.program_id(1)
    qbd = qbd_ref[...].astype(BF16)
    bias = bias_ref[...]
    n_col = heads * t_new
    key = lax.broadcasted_iota(jnp.int32, (PAGE_SIZE, n_col), 0)
    tok = lax.broadcasted_iota(jnp.int32, (PAGE_SIZE, n_col), 1) % t_new
    row = lax.broadcasted_iota(jnp.int32, (PAGE_SIZE, PAGE_SIZE), 0)
    col = lax.broadcasted_iota(jnp.int32, (PAGE_SIZE, PAGE_SIZE), 1)
    newer_mat = jnp.where(col > row, 1.0, 0.0).astype(BF16)

    def pages(loaders, masked):
        n = len(loaders)
        k = [jnp.concatenate([hk(h) for h in range(heads)], axis=1).astype(BF16) for hk, _ in loaders]
        z = [_dot(k[i], qbd) + bias for i in range(n)]
        log_keep = [-_softplus(x) for x in z]
        log_beta = [z[i] + log_keep[i] for i in range(n)]
        if masked:
            causal = key < tok
            log_keep = [jnp.where(causal, lk, 0.0) for lk in log_keep]
        split = [_split_bf16(lk) for lk in log_keep]
        within = [_dot(newer_mat, hi) + _dot(newer_mat, lo) for hi, lo in split]
        carry = carry_scr[...]
        a_t = []
        for i in range(n):
            a = jnp.exp(log_beta[i] + (within[i] + carry))
            if masked:
                a = jnp.where(causal, a, 0.0)
            a_t.append(a.T)
            carry = carry + jnp.sum(log_keep[i], axis=0, keepdims=True)
        carry_scr[...] = carry
        for h in range(heads):
            rows = slice(h * t_new, (h + 1) * t_new)
            o = _dot(a_t[0][rows, :].astype(BF16), loaders[0][1](h).astype(BF16))
            for i in range(1, n):
                o = o + _dot(a_t[i][rows, :].astype(BF16), loaders[i][1](h).astype(BF16))
            o_ref[:, h * HEAD_DIM:(h + 1) * HEAD_DIM] += o

    def head_rows(ref):
        return lambda h: ref[pl.ds(h, PAGE_SIZE, stride=heads), :]

    @pl.when(step == 0)
    def _():
        o_ref[...] = jnp.zeros_like(o_ref)
        carry_scr[...] = jnp.zeros_like(carry_scr)
        pages([(lambda h: kn_ref[:, h * HEAD_DIM:(h + 1) * HEAD_DIM],
                lambda h: vn_ref[:, h * HEAD_DIM:(h + 1) * HEAD_DIM])], True)

    pages([(head_rows(k_refs[p]), head_rows(v_refs[p])) for p in range(pages_per_step)], False)


def sb_sample_attention(q_new, k_new, v_new, cache_k, cache_v, layer, page_table, bias, t_new, pages_per_step=4):
    m, d = q_new.shape
    heads = d // HEAD_DIM
    dec_batch, n_pages = page_table.shape
    n_col = heads * t_new
    q4 = q_new.reshape(dec_batch, t_new, heads, HEAD_DIM)
    qbd = jnp.einsum('bthe,hg->bhegt', q4, jnp.eye(heads, dtype=F32)).reshape(dec_batch, d, n_col)
    bias_cols = jnp.repeat(bias.astype(F32), t_new).reshape(1, n_col)
    pad = ((0, 0), (0, PAGE_SIZE - t_new), (0, 0))
    kn = jnp.pad(k_new.reshape(dec_batch, t_new, d), pad)
    vn = jnp.pad(v_new.reshape(dec_batch, t_new, d), pad)
    steps = n_pages // pages_per_step

    def pool_rows(c):
        return c.reshape(c.shape[0], c.shape[1], PAGE_SIZE * heads, HEAD_DIM)

    def page_spec(p):
        return pl.BlockSpec(
            (None, None, PAGE_SIZE * heads, HEAD_DIM),
            lambda b, s, pt, p=p: (layer, pt[b, n_pages - 1 - (s * pages_per_step + p)], 0, 0))

    grid_spec = pltpu.PrefetchScalarGridSpec(
        num_scalar_prefetch=1,
        grid=(dec_batch, steps),
        in_specs=[
            pl.BlockSpec((None, d, n_col), lambda b, s, pt: (b, 0, 0)),
            pl.BlockSpec((1, n_col), lambda b, s, pt: (0, 0)),
            pl.BlockSpec((None, PAGE_SIZE, d), lambda b, s, pt: (b, 0, 0)),
            pl.BlockSpec((None, PAGE_SIZE, d), lambda b, s, pt: (b, 0, 0)),
        ] + [page_spec(p) for p in range(pages_per_step)] * 2,
        out_specs=pl.BlockSpec((t_new, d), lambda b, s, pt: (b, 0)),
        scratch_shapes=[pltpu.VMEM((1, n_col), F32)],
    )
    return pl.pallas_call(
        functools.partial(_sb_sample_kernel, pages_per_step=pages_per_step, t_new=t_new, heads=heads),
        grid_spec=grid_spec,
        out_shape=jax.ShapeDtypeStruct((m, d), F32),
        compiler_params=_params("arbitrary", "arbitrary"),
        name="sb_sample_attention",
    )(page_table, qbd, bias_cols, kn, vn, *([pool_rows(cache_k)] * pages_per_step),
      *([pool_rows(cache_v)] * pages_per_step))


def _alibi_slopes(heads):
    n = len(SWA_DILATIONS) * heads
    s = 2.0 ** (-8.0 * np.arange(1, n + 1) / n)
    return jnp.asarray(s.reshape(len(SWA_DILATIONS), heads), dtype=F32)


def _lane_table(per_head, width):
    heads, n = per_head.shape
    tab = jnp.pad(per_head, ((0, 0), (0, SUBLANES - n)))
    return jnp.broadcast_to(tab[:, :, None], (heads, SUBLANES, width))


def _last_rows(t, n):
    seq = t.shape[1]
    if seq >= n:
        return t[:, seq - n:]
    return jnp.pad(t, ((0, 0), (n - seq, 0), (0, 0), (0, 0)))


def kernel(x_prompt, x_sample, cache_swa_k0, cache_swa_v0, cache_swa_k1, cache_swa_v1, cache_swa_k2, cache_swa_v2, cache_sb_k, cache_sb_v, page_table, c_prompt, c_sample, w_ada, b_ada, norm_mix_g, norm_ffn_g, w_qkv_swa, w_o_swa, w_qkv_sb, w_o_sb, sb_bias, w_gate, w_up, w_down, final_norm_g):
    batch, seq, d = x_prompt.shape
    dec_batch, t_new, _ = x_sample.shape
    depth = w_ada.shape[0]
    heads = d // HEAD_DIM
    n_pages = page_table.shape[1]
    past_len = n_pages * cache_sb_k.shape[2]
    mp_rows, ms_rows = batch * seq, dec_batch * t_new
    tm_p, tm_s = 1024, ms_rows
    sb_tq, sb_hps = 256, 4

    slopes_tab = _lane_table(_alibi_slopes(heads).T, LANES)
    swa_caches = [c.reshape(c.shape[0], dec_batch, c.shape[2], d) for c in
                  (cache_swa_k0, cache_swa_v0, cache_swa_k1, cache_swa_v1, cache_swa_k2, cache_swa_v2)]

    n_c = batch + dec_batch
    c_rows = -(-n_c // SUBLANES) * SUBLANES
    c_all = jnp.pad(jnp.concatenate([c_prompt, c_sample], axis=0), ((0, c_rows - n_c), (0, 0)))
    mods = ada_all(c_all, w_ada, b_ada)

    xp = x_prompt.reshape(mp_rows, d)
    xs = x_sample.reshape(ms_rows, d)
    swa_p = [[] for _ in range(6)]
    swa_s = [[] for _ in range(6)]
    sb_p = [[], []]
    sb_s = [[], []]

    for i in range(depth):
        mod_i = mods[i].reshape(c_rows, 6, d)
        mp = [mod_i[:batch, j][:, None, :] for j in range(6)]
        ms = [jnp.repeat(mod_i[batch:n_c, j], t_new, axis=0)[None] for j in range(6)]
        hp = norm_mod(xp, norm_mix_g[i], mp[0], mp[1], tm_p)
        hs = norm_mod(xs, norm_mix_g[i], ms[0], ms[1], tm_s)
        if i % 2 == 0:
            a = i // 2
            qkv_p = project_segments(hp, w_qkv_swa, a, d, tm_p, 1024)
            qkv_s = project_segments(hs, w_qkv_swa, a, d, tm_s, 1024)
            op = swa_prompt_attention(qkv_p, slopes_tab, batch, seq)
            res = swa_sample_attention(qkv_s, swa_caches, a, slopes_tab, dec_batch, t_new, past_len)
            os_ = res[0]
            for g, win in enumerate(SWA_WINDOWS):
                for c in range(2):
                    full = qkv_p[3 * g + 1 + c].reshape(batch, seq, heads, HEAD_DIM)
                    swa_p[2 * g + c].append(_last_rows(full, win))
                    swa_s[2 * g + c].append(res[1 + 2 * g + c].reshape(dec_batch, win, heads, HEAD_DIM))
            w_o, la = w_o_swa, a
        else:
            b = i // 2
            qkv_p = project_segments(hp, w_qkv_sb, b, d, tm_p, 1024, scaled_seg0=HEAD_DIM ** -0.5)
            qkv_s = project_segments(hs, w_qkv_sb, b, d, tm_s, 1024, scaled_seg0=HEAD_DIM ** -0.5)
            op = sb_prompt_attention(qkv_p, sb_bias[b], batch, seq, sb_tq, sb_hps)
            os_ = sb_sample_attention(qkv_s[0], qkv_s[1], qkv_s[2], cache_sb_k, cache_sb_v, b,
                                      page_table, sb_bias[b], t_new)
            for c in range(2):
                sb_p[c].append(qkv_p[1 + c].reshape(batch, seq, heads, HEAD_DIM))
                sb_s[c].append(qkv_s[1 + c].reshape(dec_batch, t_new, heads, HEAD_DIM))
            w_o, la = w_o_sb, b
        xp = project_residual(op, w_o, la, xp, mp[2], tm_p, 1024)
        xs = project_residual(os_, w_o, la, xs, ms[2], tm_s, 1024)
        hp = norm_mod(xp, norm_ffn_g[i], mp[3], mp[4], tm_p)
        hs = norm_mod(xs, norm_ffn_g[i], ms[3], ms[4], tm_s)
        ap = project_swiglu(hp, w_gate, w_up, i, tm_p, 512)
        as_ = project_swiglu(hs, w_gate, w_up, i, tm_s, 512)
        xp = project_residual(ap, w_down, i, xp, mp[5], tm_p, 256)
        xs = project_residual(as_, w_down, i, xs, ms[5], tm_s, 256)

    y_prompt = final_norm(xp, final_norm_g, tm_p).reshape(batch, seq, d)
    y_sample = final_norm(xs, final_norm_g, tm_s).reshape(dec_batch, t_new, d)
    outs = [y_prompt, y_sample]
    outs += [jnp.stack(t) for t in swa_p]
    outs += [jnp.stack(t) for t in sb_p]
    outs += [jnp.stack(t) for t in swa_s]
    outs += [jnp.stack(t) for t in sb_s]
    return tuple(outs)
```

```python
import functools
import math

import numpy as np
import jax
import jax.numpy as jnp
from jax import lax
from jax.experimental import pallas as pl
from jax.experimental.pallas import tpu as pltpu

F32 = jnp.float32
BF16 = jnp.bfloat16

HEAD_DIM = 128
SWA_WINDOWS = (128, 512, 2048)
SWA_DILATIONS = (1, 4, 16)
SWA_SPAN = 128
Q_BLOCK = 128
PAGE_SIZE = 128
EPS = 1e-6
MASK_VALUE = -1e30
LANES = 128
SUBLANES = 8
VMEM_LIMIT_BYTES = 56 * 1024 * 1024

NT_DIMS = (((1,), (1,)), ((), ()))
TN_DIMS = (((0,), (0,)), ((), ()))


def _params(*sem):
    return pltpu.CompilerParams(dimension_semantics=sem, vmem_limit_bytes=VMEM_LIMIT_BYTES)


def _dot(a, b, dims=None):
    if dims is None:
        return jnp.dot(a, b, preferred_element_type=F32)
    return lax.dot_general(a, b, dims, preferred_element_type=F32)


def _softplus(z):
    return jnp.maximum(z, 0.0) + jnp.log(1.0 + jnp.exp(-jnp.abs(z)))


def _split_bf16(x):
    hi = x.astype(BF16)
    lo = (x - hi.astype(F32)).astype(BF16)
    return hi, lo


def _ada_kernel(c_ref, w_ref, b_ref, o_ref):
    c = c_ref[...]
    a = (c / (1.0 + jnp.exp(-c))).astype(BF16)
    o_ref[...] = _dot(a, w_ref[...].astype(BF16)) + b_ref[...]


def ada_all(c_all, w_ada, b_ada, tn=1024):
    depth, d, n = w_ada.shape
    r = c_all.shape[0]
    return pl.pallas_call(
        _ada_kernel,
        grid=(depth, n // tn),
        in_specs=[
            pl.BlockSpec((r, d), lambda l, j: (0, 0)),
            pl.BlockSpec((None, d, tn), lambda l, j: (l, 0, j)),
            pl.BlockSpec((None, 1, tn), lambda l, j: (l, 0, j)),
        ],
        out_specs=pl.BlockSpec((None, r, tn), lambda l, j: (l, 0, j)),
        out_shape=jax.ShapeDtypeStruct((depth, r, n), F32),
        compiler_params=_params("arbitrary", "arbitrary"),
        name="ada_mod",
    )(c_all, w_ada, b_ada.reshape(depth, 1, n))


def _rms(x):
    return x * lax.rsqrt(jnp.mean(x * x, axis=-1, keepdims=True) + EPS)


def _norm_mod_kernel(x_ref, g_ref, sh_ref, sc_ref, o_ref):
    y = _rms(x_ref[...]) * g_ref[...]
    o_ref[...] = (y * (1.0 + sc_ref[...]) + sh_ref[...]).astype(o_ref.dtype)


def _norm_kernel(x_ref, g_ref, o_ref):
    o_ref[...] = (_rms(x_ref[...]) * g_ref[...]).astype(o_ref.dtype)


def _mod_spec(mod, tiles_per_group, width, two_d):
    r = mod.shape[1]
    if two_d:
        return pl.BlockSpec((None, r, width), lambda i, j: (i // tiles_per_group, 0, j))
    return pl.BlockSpec((None, r, width), lambda i: (i // tiles_per_group, 0, 0))


def norm_mod(x, g, shift, scale, tm):
    m, d = x.shape
    tpg = (m // tm) // shift.shape[0]
    return pl.pallas_call(
        _norm_mod_kernel,
        grid=(m // tm,),
        in_specs=[
            pl.BlockSpec((tm, d), lambda i: (i, 0)),
            pl.BlockSpec((1, d), lambda i: (0, 0)),
            _mod_spec(shift, tpg, d, False),
            _mod_spec(scale, tpg, d, False),
        ],
        out_specs=pl.BlockSpec((tm, d), lambda i: (i, 0)),
        out_shape=jax.ShapeDtypeStruct((m, d), BF16),
        compiler_params=_params("arbitrary"),
        name="norm_mod",
    )(x, g.reshape(1, d), shift, scale)


def final_norm(x, g, tm):
    m, d = x.shape
    return pl.pallas_call(
        _norm_kernel,
        grid=(m // tm,),
        in_specs=[pl.BlockSpec((tm, d), lambda i: (i, 0)), pl.BlockSpec((1, d), lambda i: (0, 0))],
        out_specs=pl.BlockSpec((tm, d), lambda i: (i, 0)),
        out_shape=jax.ShapeDtypeStruct((m, d), F32),
        compiler_params=_params("arbitrary"),
        name="final_norm",
    )(x, g.reshape(1, d))


def _proj_kernel(a_ref, w_ref, o_ref, *, scaled_tiles, scale):
    acc = _dot(a_ref[...].astype(BF16), w_ref[...].astype(BF16))
    if scaled_tiles:
        acc = acc * jnp.where(pl.program_id(1) < scaled_tiles, scale, 1.0).astype(F32)
    o_ref[...] = acc


def project_segments(a, w, layer, seg, tm, tn, scaled_seg0=None):
    m, k = a.shape
    n = w.shape[2]
    tps = seg // tn
    kern = functools.partial(
        _proj_kernel,
        scaled_tiles=tps if scaled_seg0 is not None else 0,
        scale=scaled_seg0 if scaled_seg0 is not None else 1.0,
    )
    return pl.pallas_call(
        kern,
        grid=(m // tm, n // tn),
        in_specs=[
            pl.BlockSpec((tm, k), lambda i, j: (i, 0)),
            pl.BlockSpec((None, k, tn), lambda i, j: (layer, 0, j)),
        ],
        out_specs=pl.BlockSpec((None, tm, tn), lambda i, j: (j // tps, i, j % tps)),
        out_shape=jax.ShapeDtypeStruct((n // seg, m, seg), F32),
        compiler_params=_params("arbitrary", "arbitrary"),
        name="project_segments",
    )(a, w)


def _residual_kernel(a_ref, w_ref, x_ref, g_ref, o_ref):
    acc = _dot(a_ref[...].astype(BF16), w_ref[...].astype(BF16))
    o_ref[...] = x_ref[...] + g_ref[...] * acc


def project_residual(a, w, layer, x, gate, tm, tn):
    m, k = a.shape
    n = w.shape[2]
    tpg = (m // tm) // gate.shape[0]
    return pl.pallas_call(
        _residual_kernel,
        grid=(m // tm, n // tn),
        in_specs=[
            pl.BlockSpec((tm, k), lambda i, j: (i, 0)),
            pl.BlockSpec((None, k, tn), lambda i, j: (layer, 0, j)),
            pl.BlockSpec((tm, tn), lambda i, j: (i, j)),
            _mod_spec(gate, tpg, tn, True),
        ],
        out_specs=pl.BlockSpec((tm, tn), lambda i, j: (i, j)),
        out_shape=jax.ShapeDtypeStruct((m, n), F32),
        compiler_params=_params("arbitrary", "arbitrary"),
        name="project_residual",
    )(a, w, x, gate)


def _swiglu_kernel(a_ref, wg_ref, wu_ref, o_ref):
    a = a_ref[...].astype(BF16)
    g = _dot(a, wg_ref[...].astype(BF16))
    u = _dot(a, wu_ref[...].astype(BF16))
    o_ref[...] = ((g / (1.0 + jnp.exp(-g))) * u).astype(o_ref.dtype)


def project_swiglu(a, w_gate, w_up, layer, tm, tn):
    m, k = a.shape
    n = w_gate.shape[2]
    wspec = pl.BlockSpec((None, k, tn), lambda i, j: (layer, 0, j))
    return pl.pallas_call(
        _swiglu_kernel,
        grid=(m // tm, n // tn),
        in_specs=[pl.BlockSpec((tm, k), lambda i, j: (i, 0)), wspec, wspec],
        out_specs=pl.BlockSpec((tm, tn), lambda i, j: (i, j)),
        out_shape=jax.ShapeDtypeStruct((m, n), BF16),
        compiler_params=_params("arbitrary", "arbitrary"),
        name="project_swiglu",
    )(a, w_gate, w_up)


def _rows(start, dil):
    return pl.ds(start, Q_BLOCK) if dil == 1 else pl.ds(start, Q_BLOCK, stride=dil)


def _pick_unroll(n):
    for u in (4, 5, 3, 2):
        if n % u == 0:
            return u
    return 1


def _swa_prompt_kernel(sl_ref, q0, k0, v0, q1, k1, v1, q2, k2, v2, o_ref,
                       og_scr, lse_scr, bias_scr, *, seq):
    qkv = ((q0, k0, v0), (q1, k1, v1), (q2, k2, v2))
    scale = HEAD_DIM ** -0.5
    qi = lax.broadcasted_iota(jnp.int32, (Q_BLOCK, Q_BLOCK), 0)
    kj = lax.broadcasted_iota(jnp.int32, (Q_BLOCK, Q_BLOCK), 1)

    for g, dil in enumerate(SWA_DILATIONS):
        slope = sl_ref[g:g + 1, :]
        d_cur = qi - kj
        d_prev = Q_BLOCK + qi - kj
        bias_scr[g, 0] = jnp.where(d_cur >= 0, -(slope * (d_cur * dil).astype(F32)), MASK_VALUE)
        bias_scr[g, 1] = jnp.where(d_prev <= SWA_SPAN, -(slope * (d_prev * dil).astype(F32)), MASK_VALUE)

    for g, dil in enumerate(SWA_DILATIONS):
        q_ref, k_ref, v_ref = qkv[g]
        blocks_per_res = seq // (dil * Q_BLOCK)

        def blocks(starts, with_prev, g=g, dil=dil, q_ref=q_ref, k_ref=k_ref, v_ref=v_ref):
            n = len(starts)
            rows = [_rows(s, dil) for s in starts]
            q = [q_ref[r, :].astype(BF16) for r in rows]
            kc = [k_ref[r, :].astype(BF16) for r in rows]
            vc = [v_ref[r, :].astype(BF16) for r in rows]
            if with_prev:
                prow = [_rows(s - Q_BLOCK * dil, dil) for s in starts]
                kp = [k_ref[r, :].astype(BF16) for r in prow]
                vp = [v_ref[r, :].astype(BF16) for r in prow]
            s_c = [_dot(q[i], kc[i], NT_DIMS) * scale + bias_scr[g, 0] for i in range(n)]
            m = [jnp.max(s, axis=-1, keepdims=True) for s in s_c]
            if with_prev:
                s_p = [_dot(q[i], kp[i], NT_DIMS) * scale + bias_scr[g, 1] for i in range(n)]
                m = [jnp.maximum(m[i], jnp.max(s_p[i], axis=-1, keepdims=True)) for i in range(n)]
            p_c = [jnp.exp(s_c[i] - m[i]) for i in range(n)]
            den = [jnp.sum(p, axis=-1, keepdims=True) for p in p_c]
            if with_prev:
                p_p = [jnp.exp(s_p[i] - m[i]) for i in range(n)]
                den = [den[i] + jnp.sum(p_p[i], axis=-1, keepdims=True) for i in range(n)]
            inv = [1.0 / d_ for d_ in den]
            o = [_dot((p_c[i] * inv[i]).astype(BF16), vc[i]) for i in range(n)]
            if with_prev:
                o = [o[i] + _dot((p_p[i] * inv[i]).astype(BF16), vp[i]) for i in range(n)]
            lse = [jnp.broadcast_to(m[i] + jnp.log(den[i]), (Q_BLOCK, LANES)) for i in range(n)]
            for i in range(n):
                og_scr[g, rows[i], :] = o[i]
                lse_scr[g, rows[i], :] = lse[i]

        def run(n_blocks, start_of, with_prev, blocks=blocks):
            unroll = _pick_unroll(n_blocks)

            def body(it, carry):
                blocks([start_of(it * unroll + u) for u in range(unroll)], with_prev)
                return carry

            lax.fori_loop(0, n_blocks // unroll, body, 0)

        run(dil, lambda r: r, False)
        if blocks_per_res > 1:
            per = blocks_per_res - 1
            run(dil * per, lambda it, per=per, dil=dil: it // per + (it % per + 1) * (Q_BLOCK * dil), True)

    chunk = 256

    def merge_body(c, carry):
        rows = pl.ds(pl.multiple_of(c * chunk, chunk), chunk)
        l0, l1, l2 = lse_scr[0, rows, :], lse_scr[1, rows, :], lse_scr[2, rows, :]
        mx = jnp.maximum(jnp.maximum(l0, l1), l2)
        w0, w1, w2 = jnp.exp(l0 - mx), jnp.exp(l1 - mx), jnp.exp(l2 - mx)
        inv = 1.0 / (w0 + w1 + w2)
        o = (w0 * inv) * og_scr[0, rows, :] + (w1 * inv) * og_scr[1, rows, :] + (w2 * inv) * og_scr[2, rows, :]
        o_ref[rows, :] = o.astype(o_ref.dtype)
        return carry

    lax.fori_loop(0, seq // chunk, merge_body, 0)


def swa_prompt_attention(qkv, slopes_tab, batch, seq):
    _, m, d = qkv.shape
    heads = d // HEAD_DIM
    specs = [pl.BlockSpec((None, SUBLANES, LANES), lambda b, h: (h, 0, 0))]
    for s in range(9):
        specs.append(pl.BlockSpec((None, seq, HEAD_DIM), lambda b, h, s=s: (s, b, h)))
    return pl.pallas_call(
        functools.partial(_swa_prompt_kernel, seq=seq),
        grid=(batch, heads),
        in_specs=specs,
        out_specs=pl.BlockSpec((seq, HEAD_DIM), lambda b, h: (b, h)),
        out_shape=jax.ShapeDtypeStruct((m, d), BF16),
        scratch_shapes=[
            pltpu.VMEM((3, seq, HEAD_DIM), F32),
            pltpu.VMEM((3, seq, LANES), F32),
            pltpu.VMEM((3, 2, Q_BLOCK, Q_BLOCK), F32),
        ],
        compiler_params=_params("arbitrary", "arbitrary"),
        name="swa_prompt_attention",
    )(slopes_tab, *([qkv] * 9))


def _start_and_wait(copies):
    for cp in copies:
        cp.start()
    for cp in copies:
        cp.wait()


def _swa_shift_kernel(*refs, n, t_new):
    caches, outs, sem = refs[:n], refs[n:2 * n], refs[2 * n]
    copies = []
    for c in range(n):
        keep = caches[c].shape[2] - t_new
        copies.append(pltpu.make_async_copy(caches[c].at[:, :, pl.ds(t_new, keep)],
                                            outs[c].at[:, :, pl.ds(0, keep)], sem.at[2 * c]))
        copies.append(pltpu.make_async_copy(caches[c].at[:, :, pl.ds(0, t_new)],
                                            outs[c].at[:, :, pl.ds(keep, t_new)], sem.at[2 * c + 1]))
    _start_and_wait(copies)


def swa_shift_windows(caches, t_new):
    n = len(caches)
    any_spec = pl.BlockSpec(memory_space=pl.ANY)
    return pl.pallas_call(
        functools.partial(_swa_shift_kernel, n=n, t_new=t_new),
        in_specs=[any_spec] * n,
        out_specs=[any_spec] * n,
        out_shape=[jax.ShapeDtypeStruct(c.shape, c.dtype) for c in caches],
        scratch_shapes=[pltpu.SemaphoreType.DMA((2 * n,))],
        name="swa_shift_windows",
    )(*caches)


def _swa_new_rows_kernel(*refs, n, layer, t_new):
    news, outs, sem = refs[:n], refs[2 * n:3 * n], refs[3 * n]
    copies = []
    for c in range(n):
        win = outs[c].shape[2]
        copies.append(pltpu.make_async_copy(news[c], outs[c].at[layer, :, pl.ds(win - t_new, t_new)], sem.at[c]))
    _start_and_wait(copies)


def swa_write_new_rows(news, bufs, layer, t_new):
    n = len(bufs)
    any_spec = pl.BlockSpec(memory_space=pl.ANY)
    return pl.pallas_call(
        functools.partial(_swa_new_rows_kernel, n=n, layer=layer, t_new=t_new),
        in_specs=[any_spec] * (2 * n),
        out_specs=[any_spec] * n,
        out_shape=[jax.ShapeDtypeStruct(b.shape, b.dtype) for b in bufs],
        input_output_aliases={n + c: c for c in range(n)},
        scratch_shapes=[pltpu.SemaphoreType.DMA((n,))],
        name="swa_write_new_rows",
    )(*news, *bufs)


def _swa_sample_kernel(sl_ref, r_ref, k0, v0, k1, v1, k2, v2, o_ref, m_scr, l_scr, acc_scr,
                       *, t_new, heads, past_len):
    step = pl.program_id(1)
    scale = HEAD_DIM ** -0.5
    chunks = k2.shape[0]
    near = ((k0, v0), (k1, v1))

    def update(state, q_row, k, v, slope, dist, valid):
        m, l, acc = state
        s = jnp.sum(k * q_row, axis=-1, keepdims=True) * scale - slope * dist.astype(F32)
        s = jnp.where(valid, s, MASK_VALUE)
        m_new = jnp.maximum(m, jnp.max(s, axis=0, keepdims=True))
        p = jnp.exp(s - m_new)
        alpha = jnp.exp(m - m_new)
        l = l * alpha + jnp.sum(p, axis=0, keepdims=True)
        acc = acc * alpha + jnp.sum(p * v, axis=0, keepdims=True)
        return m_new, l, acc

    def far_piece(state, h, t):
        win, dil = SWA_WINDOWS[2], SWA_DILATIONS[2]
        i = step * chunks + lax.broadcasted_iota(jnp.int32, (chunks, 1), 0)
        dist = win - dil * i
        valid = (dist > 0) & (dist <= SWA_SPAN * dil) & ((past_len - win) + t + dil * i >= 0)
        q_row = r_ref[2, 0, h, t:t + 1, :]
        slope = sl_ref[h, 2:3, 0:1]
        return update(state, q_row, k2[:, t * heads + h, :], v2[:, t * heads + h, :], slope, dist, valid)

    def first_body(h, carry):
        slabs = {}
        for g, (k_ref, v_ref) in enumerate(near):
            dil = SWA_DILATIONS[g]
            for first in range(min(dil, t_new)):
                rows = pl.ds(first * heads + h, SWA_SPAN, stride=dil * heads)
                slabs[g, first] = (k_ref[rows, :], v_ref[rows, :])
        for t in range(t_new):
            state = (jnp.full((1, 1), MASK_VALUE, F32), jnp.zeros((1, 1), F32), jnp.zeros((1, HEAD_DIM), F32))
            u = lax.broadcasted_iota(jnp.int32, (t_new, 1), 0)
            for g, dil in enumerate(SWA_DILATIONS):
                dist = t - u
                valid = (dist >= 0) & ((dist & (dil - 1)) == 0)
                state = update(state, r_ref[g, 0, h, t:t + 1, :], r_ref[g, 1, h], r_ref[g, 2, h],
                               sl_ref[h, g:g + 1, 0:1], dist, valid)
            i = lax.broadcasted_iota(jnp.int32, (SWA_SPAN, 1), 0)
            for g in range(len(near)):
                win, dil = SWA_WINDOWS[g], SWA_DILATIONS[g]
                first = t % dil
                tok = first + dil * i
                dist = win + t - tok
                valid = (dist > 0) & (dist <= SWA_SPAN * dil) & ((past_len - win) + tok >= 0)
                state = update(state, r_ref[g, 0, h, t:t + 1, :], *slabs[g, first],
                               sl_ref[h, g:g + 1, 0:1], dist, valid)
            m, l, acc = far_piece(state, h, t)
            m_scr[h, t:t + 1, :] = jnp.broadcast_to(m, (1, LANES))
            l_scr[h, t:t + 1, :] = jnp.broadcast_to(l, (1, LANES))
            acc_scr[h, t:t + 1, :] = acc
        return carry

    def last_body(h, carry):
        for t in range(t_new):
            state = (m_scr[h, t:t + 1, 0:1], l_scr[h, t:t + 1, 0:1], acc_scr[h, t:t + 1, :])
            m, l, acc = far_piece(state, h, t)
            o_ref[h, t:t + 1, :] = acc / l
        return carry

    @pl.when(step == 0)
    def _():
        lax.fori_loop(0, heads, first_body, 0)

    @pl.when(step == 1)
    def _():
        lax.fori_loop(0, heads, last_body, 0)


def swa_sample_attention(qkv_new, caches, layer, slopes_tab, dec_batch, t_new, past_len):
    _, m, d = qkv_new.shape
    heads = d // HEAD_DIM
    far_dil, far_win = SWA_DILATIONS[2], SWA_WINDOWS[2]
    assert 2 * t_new <= far_dil and SWA_DILATIONS[0] == 1 and t_new <= 2 * SWA_DILATIONS[1]
    r = qkv_new.reshape(3, 3, dec_batch, t_new, heads, HEAD_DIM).transpose(0, 1, 2, 4, 3, 5)
    near = [c.reshape(c.shape[0], dec_batch, c.shape[2] * heads, HEAD_DIM) for c in caches[:4]]
    n_chunks = far_win // far_dil
    far = [c.reshape(c.shape[0], dec_batch, n_chunks, far_dil * heads, HEAD_DIM) for c in caches[4:]]
    steps = 2
    specs = [
        pl.BlockSpec((heads, SUBLANES, LANES), lambda b, s: (0, 0, 0)),
        pl.BlockSpec((3, 3, None, heads, t_new, HEAD_DIM), lambda b, s: (0, 0, b, 0, 0, 0)),
    ]
    for c in near:
        specs.append(pl.BlockSpec((None, None, c.shape[2], HEAD_DIM), lambda b, s: (layer, b, 0, 0)))
    for c in far:
        specs.append(pl.BlockSpec((None, None, n_chunks // steps, far_dil * heads // 2, HEAD_DIM),
                                  lambda b, s: (layer, b, s, 0, 0)))
    o = pl.pallas_call(
        functools.partial(_swa_sample_kernel, t_new=t_new, heads=heads, past_len=past_len),
        grid=(dec_batch, steps),
        in_specs=specs,
        out_specs=pl.BlockSpec((None, heads, t_new, HEAD_DIM), lambda b, s: (b, 0, 0, 0)),
        out_shape=jax.ShapeDtypeStruct((dec_batch, heads, t_new, HEAD_DIM), F32),
        scratch_shapes=[pltpu.VMEM((heads, t_new, LANES), F32), pltpu.VMEM((heads, t_new, LANES), F32),
                        pltpu.VMEM((heads, t_new, HEAD_DIM), F32)],
        compiler_params=_params("arbitrary", "arbitrary"),
        name="swa_sample_attention",
    )(slopes_tab, r, *near, *far)
    return o.transpose(0, 2, 1, 3).reshape(m, d)


def _sb_prompt_kernel(b_ref, q_ref, k_ref, v_ref, o_ref, *, tq, hps):
    qb = pl.program_id(2)
    row = lax.broadcasted_iota(jnp.int32, (tq, tq), 0)
    col = lax.broadcasted_iota(jnp.int32, (tq, tq), 1)
    newer_mat = jnp.where(row > col, 1.0, 0.0).astype(BF16)
    causal = col < row

    def key_block(kb, carry, acc, masked):
        heads = range(hps)
        cols = [slice(h * HEAD_DIM, (h + 1) * HEAD_DIM) for h in heads]
        ks = pl.ds(pl.multiple_of(kb * tq, tq), tq)
        q = [q_ref[:, c].astype(BF16) for c in cols]
        k = [k_ref[ks, c].astype(BF16) for c in cols]
        v = [v_ref[ks, c].astype(BF16) for c in cols]
        z = [_dot(q[h], k[h], NT_DIMS) + b_ref[h:h + 1, :] for h in heads]
        log_keep = [-_softplus(z[h]) for h in heads]
        log_beta = [z[h] + log_keep[h] for h in heads]
        if masked:
            log_keep = [jnp.where(causal, lk, 0.0) for lk in log_keep]
        split = [_split_bf16(lk) for lk in log_keep]
        newer = [_dot(split[h][0], newer_mat) + _dot(split[h][1], newer_mat) + carry[h] for h in heads]
        a = [jnp.exp(log_beta[h] + newer[h]) for h in heads]
        if masked:
            a = [jnp.where(causal, x, 0.0) for x in a]
        acc = tuple(acc[h] + _dot(a[h].astype(BF16), v[h]) for h in heads)
        carry = tuple(carry[h] + jnp.sum(log_keep[h], axis=-1, keepdims=True) for h in heads)
        return carry, acc

    carry0 = tuple(jnp.zeros((tq, 1), F32) for _ in range(hps))
    acc0 = tuple(jnp.zeros((tq, HEAD_DIM), F32) for _ in range(hps))
    state = key_block(qb, carry0, acc0, True)
    state = lax.fori_loop(0, qb, lambda it, st: key_block(qb - 1 - it, st[0], st[1], False), state)
    for h in range(hps):
        o_ref[:, h * HEAD_DIM:(h + 1) * HEAD_DIM] = state[1][h].astype(o_ref.dtype)


def sb_prompt_attention(qkv, bias, batch, seq, tq, hps):
    _, m, d = qkv.shape
    heads = d // HEAD_DIM
    nq = seq // tq
    width = hps * HEAD_DIM
    bias_tab = _lane_table(bias.astype(F32).reshape(heads // hps, hps), tq)
    return pl.pallas_call(
        functools.partial(_sb_prompt_kernel, tq=tq, hps=hps),
        grid=(batch, heads // hps, nq),
        in_specs=[
            pl.BlockSpec((None, SUBLANES, tq), lambda b, h, i: (h, 0, 0)),
            pl.BlockSpec((None, tq, width), lambda b, h, i: (0, b * nq + i, h)),
            pl.BlockSpec((None, seq, width), lambda b, h, i: (1, b, h)),
            pl.BlockSpec((None, seq, width), lambda b, h, i: (2, b, h)),
        ],
        out_specs=pl.BlockSpec((tq, width), lambda b, h, i: (b * nq + i, h)),
        out_shape=jax.ShapeDtypeStruct((m, d), BF16),
        compiler_params=_params("arbitrary", "arbitrary", "arbitrary"),
        name="sb_prompt_attention",
    )(bias_tab, qkv, qkv, qkv)


def _sb_sample_kernel(pt_ref, qbd_ref, bias_ref, kn_ref, vn_ref, *refs, pages_per_step, t_new, heads):
    del pt_ref
    k_refs = refs[:pages_per_step]
    v_refs = refs[pages_per_step:2 * pages_per_step]
    o_ref = refs[2 * pages_per_step]
    carry_scr = refs[2 * pages_per_step + 1]
    step = pl.program_id(1)
    qbd = qbd_ref[...].astype(BF16)
    bias = bias_ref[...]
    n_col = heads * t_new
    key = lax.broadcasted_iota(jnp.int32, (PAGE_SIZE, n_col), 0)
    tok = lax.broadcasted_iota(jnp.int32, (PAGE_SIZE, n_col), 1) % t_new
    row = lax.broadcasted_iota(jnp.int32, (PAGE_SIZE, PAGE_SIZE), 0)
    col = lax.broadcasted_iota(jnp.int32, (PAGE_SIZE, PAGE_SIZE), 1)
    newer_mat = jnp.where(col > row, 1.0, 0.0).astype(BF16)

    def pages(loaders, masked):
        n = len(loaders)
        k = [jnp.concatenate([hk(h) for h in range(heads)], axis=1).astype(BF16) for hk, _ in loaders]
        z = [_dot(k[i], qbd) + bias for i in range(n)]
        log_keep = [-_softplus(x) for x in z]
        log_beta = [z[i] + log_keep[i] for i in range(n)]
        if masked:
            causal = key < tok
            log_keep = [jnp.where(causal, lk, 0.0) for lk in log_keep]
        split = [_split_bf16(lk) for lk in log_keep]
        within = [_dot(newer_mat, hi) + _dot(newer_mat, lo) for hi, lo in split]
        carry = carry_scr[...]
        a_t = []
        for i in range(n):
            a = jnp.exp(log_beta[i] + (within[i] + carry))
            if masked:
                a = jnp.where(causal, a, 0.0)
            a_t.append(a.T)
            carry = carry + jnp.sum(log_keep[i], axis=0, keepdims=True)
        carry_scr[...] = carry
        for h in range(heads):
            rows = slice(h * t_new, (h + 1) * t_new)
            o = _dot(a_t[0][rows, :].astype(BF16), loaders[0][1](h).astype(BF16))
            for i in range(1, n):
                o = o + _dot(a_t[i][rows, :].astype(BF16), loaders[i][1](h).astype(BF16))
            o_ref[:, h * HEAD_DIM:(h + 1) * HEAD_DIM] += o

    def head_rows(ref):
        return lambda h: ref[pl.ds(h, PAGE_SIZE, stride=heads), :]

    @pl.when(step == 0)
    def _():
        o_ref[...] = jnp.zeros_like(o_ref)
        carry_scr[...] = jnp.zeros_like(carry_scr)
        pages([(lambda h: kn_ref[:, h * HEAD_DIM:(h + 1) * HEAD_DIM],
                lambda h: vn_ref[:, h * HEAD_DIM:(h + 1) * HEAD_DIM])], True)

    pages([(head_rows(k_refs[p]), head_rows(v_refs[p])) for p in range(pages_per_step)], False)


def sb_sample_attention(q_new, k_new, v_new, cache_k, cache_v, layer, page_table, bias, t_new, pages_per_step=4):
    m, d = q_new.shape
    heads = d // HEAD_DIM
    dec_batch, n_pages = page_table.shape
    n_col = heads * t_new
    q4 = q_new.reshape(dec_batch, t_new, heads, HEAD_DIM)
    qbd = jnp.einsum('bthe,hg->bhegt', q4, jnp.eye(heads, dtype=F32)).reshape(dec_batch, d, n_col)
    bias_cols = jnp.repeat(bias.astype(F32), t_new).reshape(1, n_col)
    pad = ((0, 0), (0, PAGE_SIZE - t_new), (0, 0))
    kn = jnp.pad(k_new.reshape(dec_batch, t_new, d), pad)
    vn = jnp.pad(v_new.reshape(dec_batch, t_new, d), pad)
    steps = n_pages // pages_per_step

    def pool_rows(c):
        return c.reshape(c.shape[0], c.shape[1], PAGE_SIZE * heads, HEAD_DIM)

    def page_spec(p):
        return pl.BlockSpec(
            (None, None, PAGE_SIZE * heads, HEAD_DIM),
            lambda b, s, pt, p=p: (layer, pt[b, n_pages - 1 - (s * pages_per_step + p)], 0, 0))

    grid_spec = pltpu.PrefetchScalarGridSpec(
        num_scalar_prefetch=1,
        grid=(dec_batch, steps),
        in_specs=[
            pl.BlockSpec((None, d, n_col), lambda b, s, pt: (b, 0, 0)),
            pl.BlockSpec((1, n_col), lambda b, s, pt: (0, 0)),
            pl.BlockSpec((None, PAGE_SIZE, d), lambda b, s, pt: (b, 0, 0)),
            pl.BlockSpec((None, PAGE_SIZE, d), lambda b, s, pt: (b, 0, 0)),
        ] + [page_spec(p) for p in range(pages_per_step)] * 2,
        out_specs=pl.BlockSpec((t_new, d), lambda b, s, pt: (b, 0)),
        scratch_shapes=[pltpu.VMEM((1, n_col), F32)],
    )
    return pl.pallas_call(
        functools.partial(_sb_sample_kernel, pages_per_step=pages_per_step, t_new=t_new, heads=heads),
        grid_spec=grid_spec,
        out_shape=jax.ShapeDtypeStruct((m, d), F32),
        compiler_params=_params("arbitrary", "arbitrary"),
        name="sb_sample_attention",
    )(page_table, qbd, bias_cols, kn, vn, *([pool_rows(cache_k)] * pages_per_step),
      *([pool_rows(cache_v)] * pages_per_step))


def _alibi_slopes(heads):
    n = len(SWA_DILATIONS) * heads
    s = 2.0 ** (-8.0 * np.arange(1, n + 1) / n)
    return jnp.asarray(s.reshape(len(SWA_DILATIONS), heads), dtype=F32)


def _lane_table(per_head, width):
    heads, n = per_head.shape
    tab = jnp.pad(per_head, ((0, 0), (0, SUBLANES - n)))
    return jnp.broadcast_to(tab[:, :, None], (heads, SUBLANES, width))


def _last_rows(t, n):
    seq = t.shape[1]
    if seq >= n:
        return t[:, seq - n:]
    return jnp.pad(t, ((0, 0), (n - seq, 0), (0, 0), (0, 0)))


def kernel(x_prompt, x_sample, cache_swa_k0, cache_swa_v0, cache_swa_k1, cache_swa_v1, cache_swa_k2, cache_swa_v2, cache_sb_k, cache_sb_v, page_table, c_prompt, c_sample, w_ada, b_ada, norm_mix_g, norm_ffn_g, w_qkv_swa, w_o_swa, w_qkv_sb, w_o_sb, sb_bias, w_gate, w_up, w_down, final_norm_g):
    batch, seq, d = x_prompt.shape
    dec_batch, t_new, _ = x_sample.shape
    depth = w_ada.shape[0]
    heads = d // HEAD_DIM
    n_pages = page_table.shape[1]
    past_len = n_pages * cache_sb_k.shape[2]
    mp_rows, ms_rows = batch * seq, dec_batch * t_new
    tm_p, tm_s = 1024, ms_rows
    sb_tq, sb_hps = 256, 4

    slopes_tab = _lane_table(_alibi_slopes(heads).T, LANES)
    swa_caches = (cache_swa_k0, cache_swa_v0, cache_swa_k1, cache_swa_v1, cache_swa_k2, cache_swa_v2)
    swa_s = swa_shift_windows(swa_caches, t_new)

    n_c = batch + dec_batch
    c_rows = -(-n_c // SUBLANES) * SUBLANES
    c_all = jnp.pad(jnp.concatenate([c_prompt, c_sample], axis=0), ((0, c_rows - n_c), (0, 0)))
    mods = ada_all(c_all, w_ada, b_ada)

    xp = x_prompt.reshape(mp_rows, d)
    xs = x_sample.reshape(ms_rows, d)
    swa_p = [[] for _ in range(6)]
    sb_p = [[], []]
    sb_s = [[], []]

    for i in range(depth):
        mod_i = mods[i].reshape(c_rows, 6, d)
        mp = [mod_i[:batch, j][:, None, :] for j in range(6)]
        ms = [jnp.repeat(mod_i[batch:n_c, j], t_new, axis=0)[None] for j in range(6)]
        hp = norm_mod(xp, norm_mix_g[i], mp[0], mp[1], tm_p)
        hs = norm_mod(xs, norm_mix_g[i], ms[0], ms[1], tm_s)
        if i % 2 == 0:
            a = i // 2
            qkv_p = project_segments(hp, w_qkv_swa, a, d, tm_p, 1024)
            qkv_s = project_segments(hs, w_qkv_swa, a, d, tm_s, 1024)
            op = swa_prompt_attention(qkv_p, slopes_tab, batch, seq)
            os_ = swa_sample_attention(qkv_s, swa_caches, a, slopes_tab, dec_batch, t_new, past_len)
            news = [qkv_s[3 * g + 1 + c].reshape(dec_batch, t_new, heads, HEAD_DIM)
                    for g in range(len(SWA_WINDOWS)) for c in range(2)]
            swa_s = swa_write_new_rows(news, swa_s, a, t_new)
            for g, win in enumerate(SWA_WINDOWS):
                for c in range(2):
                    full = qkv_p[3 * g + 1 + c].reshape(batch, seq, heads, HEAD_DIM)
                    swa_p[2 * g + c].append(_last_rows(full, win))
            w_o, la = w_o_swa, a
        else:
            b = i // 2
            qkv_p = project_segments(hp, w_qkv_sb, b, d, tm_p, 1024, scaled_seg0=HEAD_DIM ** -0.5)
            qkv_s = project_segments(hs, w_qkv_sb, b, d, tm_s, 1024, scaled_seg0=HEAD_DIM ** -0.5)
            op = sb_prompt_attention(qkv_p, sb_bias[b], batch, seq, sb_tq, sb_hps)
            os_ = sb_sample_attention(qkv_s[0], qkv_s[1], qkv_s[2], cache_sb_k, cache_sb_v, b,
                                      page_table, sb_bias[b], t_new)
            for c in range(2):
                sb_p[c].append(qkv_p[1 + c].reshape(batch, seq, heads, HEAD_DIM))
                sb_s[c].append(qkv_s[1 + c].reshape(dec_batch, t_new, heads, HEAD_DIM))
            w_o, la = w_o_sb, b
        xp = project_residual(op, w_o, la, xp, mp[2], tm_p, 1024)
        xs = project_residual(os_, w_o, la, xs, ms[2], tm_s, 1024)
        hp = norm_mod(xp, norm_ffn_g[i], mp[3], mp[4], tm_p)
        hs = norm_mod(xs, norm_ffn_g[i], ms[3], ms[4], tm_s)
        ap = project_swiglu(hp, w_gate, w_up, i, tm_p, 512)
        as_ = project_swiglu(hs, w_gate, w_up, i, tm_s, 512)
        xp = project_residual(ap, w_down, i, xp, mp[5], tm_p, 256)
        xs = project_residual(as_, w_down, i, xs, ms[5], tm_s, 256)

    y_prompt = final_norm(xp, final_norm_g, tm_p).reshape(batch, seq, d)
    y_sample = final_norm(xs, final_norm_g, tm_s).reshape(dec_batch, t_new, d)
    outs = [y_prompt, y_sample]
    outs += [jnp.stack(t) for t in swa_p]
    outs += [jnp.stack(t) for t in sb_p]
    outs += list(swa_s)
    outs += [jnp.stack(t) for t in sb_s]
    return tuple(outs)
```

```python
import functools
import math

import numpy as np
import jax
import jax.numpy as jnp
from jax import lax
from jax.experimental import pallas as pl
from jax.experimental.pallas import tpu as pltpu

F32 = jnp.float32
BF16 = jnp.bfloat16

HEAD_DIM = 128
SWA_WINDOWS = (128, 512, 2048)
SWA_DILATIONS = (1, 4, 16)
SWA_SPAN = 128
Q_BLOCK = 128
PAGE_SIZE = 128
EPS = 1e-6
MASK_VALUE = -1e30
LANES = 128
SUBLANES = 8
VMEM_LIMIT_BYTES = 56 * 1024 * 1024

NT_DIMS = (((1,), (1,)), ((), ()))
TN_DIMS = (((0,), (0,)), ((), ()))


def _params(*sem):
    return pltpu.CompilerParams(dimension_semantics=sem, vmem_limit_bytes=VMEM_LIMIT_BYTES)


def _dot(a, b, dims=None):
    if dims is None:
        return jnp.dot(a, b, preferred_element_type=F32)
    return lax.dot_general(a, b, dims, preferred_element_type=F32)


def _softplus(z):
    return jnp.maximum(z, 0.0) + jnp.log(1.0 + jnp.exp(-jnp.abs(z)))


def _split_bf16(x):
    hi = x.astype(BF16)
    lo = (x - hi.astype(F32)).astype(BF16)
    return hi, lo


def _ada_kernel(c_ref, w_ref, b_ref, o_ref):
    c = c_ref[...]
    a = (c / (1.0 + jnp.exp(-c))).astype(BF16)
    o_ref[...] = _dot(a, w_ref[...].astype(BF16)) + b_ref[...]


def ada_all(c_all, w_ada, b_ada, tn=1024):
    depth, d, n = w_ada.shape
    r = c_all.shape[0]
    return pl.pallas_call(
        _ada_kernel,
        grid=(depth, n // tn),
        in_specs=[
            pl.BlockSpec((r, d), lambda l, j: (0, 0)),
            pl.BlockSpec((None, d, tn), lambda l, j: (l, 0, j)),
            pl.BlockSpec((None, 1, tn), lambda l, j: (l, 0, j)),
        ],
        out_specs=pl.BlockSpec((None, r, tn), lambda l, j: (l, 0, j)),
        out_shape=jax.ShapeDtypeStruct((depth, r, n), F32),
        compiler_params=_params("arbitrary", "arbitrary"),
        name="ada_mod",
    )(c_all, w_ada, b_ada.reshape(depth, 1, n))


def _rms(x):
    return x * lax.rsqrt(jnp.mean(x * x, axis=-1, keepdims=True) + EPS)


def _norm_mod_kernel(x_ref, g_ref, sh_ref, sc_ref, o_ref):
    y = _rms(x_ref[...]) * g_ref[...]
    o_ref[...] = (y * (1.0 + sc_ref[...]) + sh_ref[...]).astype(o_ref.dtype)


def _norm_kernel(x_ref, g_ref, o_ref):
    o_ref[...] = (_rms(x_ref[...]) * g_ref[...]).astype(o_ref.dtype)


def _mod_spec(mod, tiles_per_group, width, two_d):
    r = mod.shape[1]
    if two_d:
        return pl.BlockSpec((None, r, width), lambda i, j: (i // tiles_per_group, 0, j))
    return pl.BlockSpec((None, r, width), lambda i: (i // tiles_per_group, 0, 0))


def norm_mod(x, g, shift, scale, tm):
    m, d = x.shape
    tpg = (m // tm) // shift.shape[0]
    return pl.pallas_call(
        _norm_mod_kernel,
        grid=(m // tm,),
        in_specs=[
            pl.BlockSpec((tm, d), lambda i: (i, 0)),
            pl.BlockSpec((1, d), lambda i: (0, 0)),
            _mod_spec(shift, tpg, d, False),
            _mod_spec(scale, tpg, d, False),
        ],
        out_specs=pl.BlockSpec((tm, d), lambda i: (i, 0)),
        out_shape=jax.ShapeDtypeStruct((m, d), BF16),
        compiler_params=_params("arbitrary"),
        name="norm_mod",
    )(x, g.reshape(1, d), shift, scale)


def final_norm(x, g, tm):
    m, d = x.shape
    return pl.pallas_call(
        _norm_kernel,
        grid=(m // tm,),
        in_specs=[pl.BlockSpec((tm, d), lambda i: (i, 0)), pl.BlockSpec((1, d), lambda i: (0, 0))],
        out_specs=pl.BlockSpec((tm, d), lambda i: (i, 0)),
        out_shape=jax.ShapeDtypeStruct((m, d), F32),
        compiler_params=_params("arbitrary"),
        name="final_norm",
    )(x, g.reshape(1, d))


def _proj_kernel(a_ref, w_ref, o_ref, *, scaled_tiles, scale):
    acc = _dot(a_ref[...].astype(BF16), w_ref[...].astype(BF16))
    if scaled_tiles:
        acc = acc * jnp.where(pl.program_id(1) < scaled_tiles, scale, 1.0).astype(F32)
    o_ref[...] = acc


def project_segments(a, w, layer, seg, tm, tn, scaled_seg0=None):
    m, k = a.shape
    n = w.shape[2]
    tps = seg // tn
    kern = functools.partial(
        _proj_kernel,
        scaled_tiles=tps if scaled_seg0 is not None else 0,
        scale=scaled_seg0 if scaled_seg0 is not None else 1.0,
    )
    return pl.pallas_call(
        kern,
        grid=(m // tm, n // tn),
        in_specs=[
            pl.BlockSpec((tm, k), lambda i, j: (i, 0)),
            pl.BlockSpec((None, k, tn), lambda i, j: (layer, 0, j)),
        ],
        out_specs=pl.BlockSpec((None, tm, tn), lambda i, j: (j // tps, i, j % tps)),
        out_shape=jax.ShapeDtypeStruct((n // seg, m, seg), F32),
        compiler_params=_params("arbitrary", "arbitrary"),
        name="project_segments",
    )(a, w)


def _residual_kernel(a_ref, w_ref, x_ref, g_ref, o_ref):
    acc = _dot(a_ref[...].astype(BF16), w_ref[...].astype(BF16))
    o_ref[...] = x_ref[...] + g_ref[...] * acc


def project_residual(a, w, layer, x, gate, tm, tn):
    m, k = a.shape
    n = w.shape[2]
    tpg = (m // tm) // gate.shape[0]
    return pl.pallas_call(
        _residual_kernel,
        grid=(m // tm, n // tn),
        in_specs=[
            pl.BlockSpec((tm, k), lambda i, j: (i, 0)),
            pl.BlockSpec((None, k, tn), lambda i, j: (layer, 0, j)),
            pl.BlockSpec((tm, tn), lambda i, j: (i, j)),
            _mod_spec(gate, tpg, tn, True),
        ],
        out_specs=pl.BlockSpec((tm, tn), lambda i, j: (i, j)),
        out_shape=jax.ShapeDtypeStruct((m, n), F32),
        compiler_params=_params("arbitrary", "arbitrary"),
        name="project_residual",
    )(a, w, x, gate)


def _swiglu_kernel(a_ref, wg_ref, wu_ref, o_ref):
    a = a_ref[...].astype(BF16)
    g = _dot(a, wg_ref[...].astype(BF16))
    u = _dot(a, wu_ref[...].astype(BF16))
    o_ref[...] = ((g / (1.0 + jnp.exp(-g))) * u).astype(o_ref.dtype)


def project_swiglu(a, w_gate, w_up, layer, tm, tn):
    m, k = a.shape
    n = w_gate.shape[2]
    wspec = pl.BlockSpec((None, k, tn), lambda i, j: (layer, 0, j))
    return pl.pallas_call(
        _swiglu_kernel,
        grid=(m // tm, n // tn),
        in_specs=[pl.BlockSpec((tm, k), lambda i, j: (i, 0)), wspec, wspec],
        out_specs=pl.BlockSpec((tm, tn), lambda i, j: (i, j)),
        out_shape=jax.ShapeDtypeStruct((m, n), BF16),
        compiler_params=_params("arbitrary", "arbitrary"),
        name="project_swiglu",
    )(a, w_gate, w_up)


def _rows(start, dil):
    return pl.ds(start, Q_BLOCK) if dil == 1 else pl.ds(start, Q_BLOCK, stride=dil)


def _pick_unroll(n):
    for u in (4, 5, 3, 2):
        if n % u == 0:
            return u
    return 1


def _swa_prompt_kernel(sl_ref, q0, k0, v0, q1, k1, v1, q2, k2, v2, o_ref,
                       og_scr, lse_scr, bias_scr, *, seq):
    qkv = ((q0, k0, v0), (q1, k1, v1), (q2, k2, v2))
    scale = HEAD_DIM ** -0.5
    qi = lax.broadcasted_iota(jnp.int32, (Q_BLOCK, Q_BLOCK), 0)
    kj = lax.broadcasted_iota(jnp.int32, (Q_BLOCK, Q_BLOCK), 1)

    for g, dil in enumerate(SWA_DILATIONS):
        slope = sl_ref[g:g + 1, :]
        d_cur = qi - kj
        d_prev = Q_BLOCK + qi - kj
        bias_scr[g, 0] = jnp.where(d_cur >= 0, -(slope * (d_cur * dil).astype(F32)), MASK_VALUE)
        bias_scr[g, 1] = jnp.where(d_prev <= SWA_SPAN, -(slope * (d_prev * dil).astype(F32)), MASK_VALUE)

    for g, dil in enumerate(SWA_DILATIONS):
        q_ref, k_ref, v_ref = qkv[g]
        blocks_per_res = seq // (dil * Q_BLOCK)

        def blocks(starts, with_prev, g=g, dil=dil, q_ref=q_ref, k_ref=k_ref, v_ref=v_ref):
            n = len(starts)
            rows = [_rows(s, dil) for s in starts]
            q = [q_ref[r, :].astype(BF16) for r in rows]
            kc = [k_ref[r, :].astype(BF16) for r in rows]
            vc = [v_ref[r, :].astype(BF16) for r in rows]
            if with_prev:
                prow = [_rows(s - Q_BLOCK * dil, dil) for s in starts]
                kp = [k_ref[r, :].astype(BF16) for r in prow]
                vp = [v_ref[r, :].astype(BF16) for r in prow]
            s_c = [_dot(q[i], kc[i], NT_DIMS) * scale + bias_scr[g, 0] for i in range(n)]
            m = [jnp.max(s, axis=-1, keepdims=True) for s in s_c]
            if with_prev:
                s_p = [_dot(q[i], kp[i], NT_DIMS) * scale + bias_scr[g, 1] for i in range(n)]
                m = [jnp.maximum(m[i], jnp.max(s_p[i], axis=-1, keepdims=True)) for i in range(n)]
            p_c = [jnp.exp(s_c[i] - m[i]) for i in range(n)]
            den = [jnp.sum(p, axis=-1, keepdims=True) for p in p_c]
            if with_prev:
                p_p = [jnp.exp(s_p[i] - m[i]) for i in range(n)]
                den = [den[i] + jnp.sum(p_p[i], axis=-1, keepdims=True) for i in range(n)]
            inv = [1.0 / d_ for d_ in den]
            o = [_dot((p_c[i] * inv[i]).astype(BF16), vc[i]) for i in range(n)]
            if with_prev:
                o = [o[i] + _dot((p_p[i] * inv[i]).astype(BF16), vp[i]) for i in range(n)]
            lse = [jnp.broadcast_to(m[i] + jnp.log(den[i]), (Q_BLOCK, LANES)) for i in range(n)]
            for i in range(n):
                og_scr[g, rows[i], :] = o[i]
                lse_scr[g, rows[i], :] = lse[i]

        def run(n_blocks, start_of, with_prev, blocks=blocks):
            unroll = _pick_unroll(n_blocks)

            def body(it, carry):
                blocks([start_of(it * unroll + u) for u in range(unroll)], with_prev)
                return carry

            lax.fori_loop(0, n_blocks // unroll, body, 0)

        run(dil, lambda r: r, False)
        if blocks_per_res > 1:
            per = blocks_per_res - 1
            run(dil * per, lambda it, per=per, dil=dil: it // per + (it % per + 1) * (Q_BLOCK * dil), True)

    chunk = 256

    def merge_body(c, carry):
        rows = pl.ds(pl.multiple_of(c * chunk, chunk), chunk)
        l0, l1, l2 = lse_scr[0, rows, :], lse_scr[1, rows, :], lse_scr[2, rows, :]
        mx = jnp.maximum(jnp.maximum(l0, l1), l2)
        w0, w1, w2 = jnp.exp(l0 - mx), jnp.exp(l1 - mx), jnp.exp(l2 - mx)
        inv = 1.0 / (w0 + w1 + w2)
        o = (w0 * inv) * og_scr[0, rows, :] + (w1 * inv) * og_scr[1, rows, :] + (w2 * inv) * og_scr[2, rows, :]
        o_ref[rows, :] = o.astype(o_ref.dtype)
        return carry

    lax.fori_loop(0, seq // chunk, merge_body, 0)


def swa_prompt_attention(qkv, slopes_tab, batch, seq):
    _, m, d = qkv.shape
    heads = d // HEAD_DIM
    specs = [pl.BlockSpec((None, SUBLANES, LANES), lambda b, h: (h, 0, 0))]
    for s in range(9):
        specs.append(pl.BlockSpec((None, seq, HEAD_DIM), lambda b, h, s=s: (s, b, h)))
    return pl.pallas_call(
        functools.partial(_swa_prompt_kernel, seq=seq),
        grid=(batch, heads),
        in_specs=specs,
        out_specs=pl.BlockSpec((seq, HEAD_DIM), lambda b, h: (b, h)),
        out_shape=jax.ShapeDtypeStruct((m, d), BF16),
        scratch_shapes=[
            pltpu.VMEM((3, seq, HEAD_DIM), F32),
            pltpu.VMEM((3, seq, LANES), F32),
            pltpu.VMEM((3, 2, Q_BLOCK, Q_BLOCK), F32),
        ],
        compiler_params=_params("arbitrary", "arbitrary"),
        name="swa_prompt_attention",
    )(slopes_tab, *([qkv] * 9))


SHIFT_BLOCK_ROWS = 8192


def _shift_rows_kernel(x_ref, o_ref, *, shift):
    last = pl.program_id(0) == pl.num_programs(0) - 1

    @pl.when(jnp.logical_not(last))
    def _():
        o_ref[...] = x_ref[...]

    @pl.when(last)
    def _():
        rows = o_ref.shape[0]
        o_ref[0:rows - shift, :] = x_ref[shift:rows, :]
        o_ref[rows - shift:rows, :] = x_ref[rows - shift:rows, :]


def swa_shift_window(cache, t_new):
    layers, dec_batch, win, heads, hd = cache.shape
    total = layers * dec_batch * win * heads
    shift = t_new * heads
    rows = min(SHIFT_BLOCK_ROWS, total)
    assert rows % shift == 0 and total % rows == 0
    per_block, last_start = rows // shift, (total - rows) // shift
    out = pl.pallas_call(
        functools.partial(_shift_rows_kernel, shift=shift),
        grid=(total // rows,),
        in_specs=[pl.BlockSpec((pl.Element(rows), pl.Element(hd)),
                               lambda i: (jnp.minimum(i * per_block + 1, last_start) * shift, 0))],
        out_specs=pl.BlockSpec((rows, hd), lambda i: (i, 0)),
        out_shape=jax.ShapeDtypeStruct((total, hd), cache.dtype),
        compiler_params=_params("arbitrary"),
        name="swa_shift_window",
    )(cache.reshape(total, hd))
    return out.reshape(cache.shape)


def _new_rows_kernel(*refs, n):
    news, outs = refs[:n], refs[2 * n:]
    for c in range(n):
        outs[c][...] = news[c][...]


def swa_write_new_rows(news, bufs, layer, t_new):
    n = len(bufs)
    dec_batch, _, heads, hd = news[0].shape
    specs = [pl.BlockSpec((None, t_new, heads, hd), lambda b: (b, 0, 0, 0))] * n
    specs += [pl.BlockSpec(memory_space=pl.ANY)] * n
    out_specs = [pl.BlockSpec((None, None, t_new, heads, hd),
                              lambda b, last=buf.shape[2] // t_new - 1: (layer, b, last, 0, 0)) for buf in bufs]
    return pl.pallas_call(
        functools.partial(_new_rows_kernel, n=n),
        grid=(dec_batch,),
        in_specs=specs,
        out_specs=out_specs,
        out_shape=[jax.ShapeDtypeStruct(b.shape, b.dtype) for b in bufs],
        input_output_aliases={n + c: c for c in range(n)},
        compiler_params=_params("arbitrary"),
        name="swa_write_new_rows",
    )(*news, *bufs)


def _swa_sample_kernel(sl_ref, r_ref, k0, v0, k1, v1, k2, v2, o_ref, m_scr, l_scr, acc_scr,
                       *, t_new, heads, past_len):
    step = pl.program_id(1)
    scale = HEAD_DIM ** -0.5
    chunks = k2.shape[0]
    near = ((k0, v0), (k1, v1))

    def update(state, q_row, k, v, slope, dist, valid):
        m, l, acc = state
        s = jnp.sum(k * q_row, axis=-1, keepdims=True) * scale - slope * dist.astype(F32)
        s = jnp.where(valid, s, MASK_VALUE)
        m_new = jnp.maximum(m, jnp.max(s, axis=0, keepdims=True))
        p = jnp.exp(s - m_new)
        alpha = jnp.exp(m - m_new)
        l = l * alpha + jnp.sum(p, axis=0, keepdims=True)
        acc = acc * alpha + jnp.sum(p * v, axis=0, keepdims=True)
        return m_new, l, acc

    def far_piece(state, h, t):
        win, dil = SWA_WINDOWS[2], SWA_DILATIONS[2]
        i = step * chunks + lax.broadcasted_iota(jnp.int32, (chunks, 1), 0)
        dist = win - dil * i
        valid = (dist > 0) & (dist <= SWA_SPAN * dil) & ((past_len - win) + t + dil * i >= 0)
        q_row = r_ref[2, 0, h, t:t + 1, :]
        slope = sl_ref[h, 2:3, 0:1]
        return update(state, q_row, k2[:, t * heads + h, :], v2[:, t * heads + h, :], slope, dist, valid)

    def first_body(h, carry):
        slabs = {}
        for g, (k_ref, v_ref) in enumerate(near):
            dil = SWA_DILATIONS[g]
            for first in range(min(dil, t_new)):
                rows = pl.ds(first * heads + h, SWA_SPAN, stride=dil * heads)
                slabs[g, first] = (k_ref[rows, :], v_ref[rows, :])
        for t in range(t_new):
            state = (jnp.full((1, 1), MASK_VALUE, F32), jnp.zeros((1, 1), F32), jnp.zeros((1, HEAD_DIM), F32))
            u = lax.broadcasted_iota(jnp.int32, (t_new, 1), 0)
            for g, dil in enumerate(SWA_DILATIONS):
                dist = t - u
                valid = (dist >= 0) & ((dist & (dil - 1)) == 0)
                state = update(state, r_ref[g, 0, h, t:t + 1, :], r_ref[g, 1, h], r_ref[g, 2, h],
                               sl_ref[h, g:g + 1, 0:1], dist, valid)
            i = lax.broadcasted_iota(jnp.int32, (SWA_SPAN, 1), 0)
            for g in range(len(near)):
                win, dil = SWA_WINDOWS[g], SWA_DILATIONS[g]
                first = t % dil
                tok = first + dil * i
                dist = win + t - tok
                valid = (dist > 0) & (dist <= SWA_SPAN * dil) & ((past_len - win) + tok >= 0)
                state = update(state, r_ref[g, 0, h, t:t + 1, :], *slabs[g, first],
                               sl_ref[h, g:g + 1, 0:1], dist, valid)
            m, l, acc = far_piece(state, h, t)
            m_scr[h, t:t + 1, :] = jnp.broadcast_to(m, (1, LANES))
            l_scr[h, t:t + 1, :] = jnp.broadcast_to(l, (1, LANES))
            acc_scr[h, t:t + 1, :] = acc
        return carry

    def last_body(h, carry):
        for t in range(t_new):
            state = (m_scr[h, t:t + 1, 0:1], l_scr[h, t:t + 1, 0:1], acc_scr[h, t:t + 1, :])
            m, l, acc = far_piece(state, h, t)
            o_ref[h, t:t + 1, :] = acc / l
        return carry

    @pl.when(step == 0)
    def _():
        lax.fori_loop(0, heads, first_body, 0)

    @pl.when(step == 1)
    def _():
        lax.fori_loop(0, heads, last_body, 0)


def swa_sample_attention(qkv_new, caches, layer, slopes_tab, dec_batch, t_new, past_len):
    _, m, d = qkv_new.shape
    heads = d // HEAD_DIM
    far_dil, far_win = SWA_DILATIONS[2], SWA_WINDOWS[2]
    assert 2 * t_new <= far_dil and SWA_DILATIONS[0] == 1 and t_new <= 2 * SWA_DILATIONS[1]
    r = qkv_new.reshape(3, 3, dec_batch, t_new, heads, HEAD_DIM).transpose(0, 1, 2, 4, 3, 5)
    near = [c.reshape(c.shape[0], dec_batch, c.shape[2] * heads, HEAD_DIM) for c in caches[:4]]
    n_chunks = far_win // far_dil
    far = [c.reshape(c.shape[0], dec_batch, n_chunks, far_dil * heads, HEAD_DIM) for c in caches[4:]]
    steps = 2
    specs = [
        pl.BlockSpec((heads, SUBLANES, LANES), lambda b, s: (0, 0, 0)),
        pl.BlockSpec((3, 3, None, heads, t_new, HEAD_DIM), lambda b, s: (0, 0, b, 0, 0, 0)),
    ]
    for c in near:
        specs.append(pl.BlockSpec((None, None, c.shape[2], HEAD_DIM), lambda b, s: (layer, b, 0, 0)))
    for c in far:
        specs.append(pl.BlockSpec((None, None, n_chunks // steps, far_dil * heads // 2, HEAD_DIM),
                                  lambda b, s: (layer, b, s, 0, 0)))
    o = pl.pallas_call(
        functools.partial(_swa_sample_kernel, t_new=t_new, heads=heads, past_len=past_len),
        grid=(dec_batch, steps),
        in_specs=specs,
        out_specs=pl.BlockSpec((None, heads, t_new, HEAD_DIM), lambda b, s: (b, 0, 0, 0)),
        out_shape=jax.ShapeDtypeStruct((dec_batch, heads, t_new, HEAD_DIM), F32),
        scratch_shapes=[pltpu.VMEM((heads, t_new, LANES), F32), pltpu.VMEM((heads, t_new, LANES), F32),
                        pltpu.VMEM((heads, t_new, HEAD_DIM), F32)],
        compiler_params=_params("arbitrary", "arbitrary"),
        name="swa_sample_attention",
    )(slopes_tab, r, *near, *far)
    return o.transpose(0, 2, 1, 3).reshape(m, d)


def _sb_prompt_kernel(b_ref, q_ref, k_ref, v_ref, o_ref, *, tq, hps):
    qb = pl.program_id(2)
    row = lax.broadcasted_iota(jnp.int32, (tq, tq), 0)
    col = lax.broadcasted_iota(jnp.int32, (tq, tq), 1)
    newer_mat = jnp.where(row > col, 1.0, 0.0).astype(BF16)
    causal = col < row

    def key_block(kb, carry, acc, masked):
        heads = range(hps)
        cols = [slice(h * HEAD_DIM, (h + 1) * HEAD_DIM) for h in heads]
        ks = pl.ds(pl.multiple_of(kb * tq, tq), tq)
        q = [q_ref[:, c].astype(BF16) for c in cols]
        k = [k_ref[ks, c].astype(BF16) for c in cols]
        v = [v_ref[ks, c].astype(BF16) for c in cols]
        z = [_dot(q[h], k[h], NT_DIMS) + b_ref[h:h + 1, :] for h in heads]
        log_keep = [-_softplus(z[h]) for h in heads]
        log_beta = [z[h] + log_keep[h] for h in heads]
        if masked:
            log_keep = [jnp.where(causal, lk, 0.0) for lk in log_keep]
        split = [_split_bf16(lk) for lk in log_keep]
        newer = [_dot(split[h][0], newer_mat) + _dot(split[h][1], newer_mat) + carry[h] for h in heads]
        a = [jnp.exp(log_beta[h] + newer[h]) for h in heads]
        if masked:
            a = [jnp.where(causal, x, 0.0) for x in a]
        acc = tuple(acc[h] + _dot(a[h].astype(BF16), v[h]) for h in heads)
        carry = tuple(carry[h] + jnp.sum(log_keep[h], axis=-1, keepdims=True) for h in heads)
        return carry, acc

    carry0 = tuple(jnp.zeros((tq, 1), F32) for _ in range(hps))
    acc0 = tuple(jnp.zeros((tq, HEAD_DIM), F32) for _ in range(hps))
    state = key_block(qb, carry0, acc0, True)
    state = lax.fori_loop(0, qb, lambda it, st: key_block(qb - 1 - it, st[0], st[1], False), state)
    for h in range(hps):
        o_ref[:, h * HEAD_DIM:(h + 1) * HEAD_DIM] = state[1][h].astype(o_ref.dtype)


def sb_prompt_attention(qkv, bias, batch, seq, tq, hps):
    _, m, d = qkv.shape
    heads = d // HEAD_DIM
    nq = seq // tq
    width = hps * HEAD_DIM
    bias_tab = _lane_table(bias.astype(F32).reshape(heads // hps, hps), tq)
    return pl.pallas_call(
        functools.partial(_sb_prompt_kernel, tq=tq, hps=hps),
        grid=(batch, heads // hps, nq),
        in_specs=[
            pl.BlockSpec((None, SUBLANES, tq), lambda b, h, i: (h, 0, 0)),
            pl.BlockSpec((None, tq, width), lambda b, h, i: (0, b * nq + i, h)),
            pl.BlockSpec((None, seq, width), lambda b, h, i: (1, b, h)),
            pl.BlockSpec((None, seq, width), lambda b, h, i: (2, b, h)),
        ],
        out_specs=pl.BlockSpec((tq, width), lambda b, h, i: (b * nq + i, h)),
        out_shape=jax.ShapeDtypeStruct((m, d), BF16),
        compiler_params=_params("arbitrary", "arbitrary", "arbitrary"),
        name="sb_prompt_attention",
    )(bias_tab, qkv, qkv, qkv)


def _sb_sample_kernel(pt_ref, qbd_ref, bias_ref, kn_ref, vn_ref, *refs, pages_per_step, t_new, heads):
    del pt_ref
    k_refs = refs[:pages_per_step]
    v_refs = refs[pages_per_step:2 * pages_per_step]
    o_ref = refs[2 * pages_per_step]
    carry_scr = refs[2 * pages_per_step + 1]
    step = pl.program_id(1)
    qbd = qbd_ref[...].astype(BF16)
    bias = bias_ref[...]
    n_col = heads * t_new
    key = lax.broadcasted_iota(jnp.int32, (PAGE_SIZE, n_col), 0)
    tok = lax.broadcasted_iota(jnp.int32, (PAGE_SIZE, n_col), 1) % t_new
    row = lax.broadcasted_iota(jnp.int32, (PAGE_SIZE, PAGE_SIZE), 0)
    col = lax.broadcasted_iota(jnp.int32, (PAGE_SIZE, PAGE_SIZE), 1)
    newer_mat = jnp.where(col > row, 1.0, 0.0).astype(BF16)

    def pages(loaders, masked):
        n = len(loaders)
        k = [jnp.concatenate([hk(h) for h in range(heads)], axis=1).astype(BF16) for hk, _ in loaders]
        z = [_dot(k[i], qbd) + bias for i in range(n)]
        log_keep = [-_softplus(x) for x in z]
        log_beta = [z[i] + log_keep[i] for i in range(n)]
        if masked:
            causal = key < tok
            log_keep = [jnp.where(causal, lk, 0.0) for lk in log_keep]
        split = [_split_bf16(lk) for lk in log_keep]
        within = [_dot(newer_mat, hi) + _dot(newer_mat, lo) for hi, lo in split]
        carry = carry_scr[...]
        a_t = []
        for i in range(n):
            a = jnp.exp(log_beta[i] + (within[i] + carry))
            if masked:
                a = jnp.where(causal, a, 0.0)
            a_t.append(a.T)
            carry = carry + jnp.sum(log_keep[i], axis=0, keepdims=True)
        carry_scr[...] = carry
        for h in range(heads):
            rows = slice(h * t_new, (h + 1) * t_new)
            o = _dot(a_t[0][rows, :].astype(BF16), loaders[0][1](h).astype(BF16))
            for i in range(1, n):
                o = o + _dot(a_t[i][rows, :].astype(BF16), loaders[i][1](h).astype(BF16))
            o_ref[:, h * HEAD_DIM:(h + 1) * HEAD_DIM] += o

    def head_rows(ref):
        return lambda h: ref[pl.ds(h, PAGE_SIZE, stride=heads), :]

    @pl.when(step == 0)
    def _():
        o_ref[...] = jnp.zeros_like(o_ref)
        carry_scr[...] = jnp.zeros_like(carry_scr)
        pages([(lambda h: kn_ref[:, h * HEAD_DIM:(h + 1) * HEAD_DIM],
                lambda h: vn_ref[:, h * HEAD_DIM:(h + 1) * HEAD_DIM])], True)

    pages([(head_rows(k_refs[p]), head_rows(v_refs[p])) for p in range(pages_per_step)], False)


def sb_sample_attention(q_new, k_new, v_new, cache_k, cache_v, layer, page_table, bias, t_new, pages_per_step=4):
    m, d = q_new.shape
    heads = d // HEAD_DIM
    dec_batch, n_pages = page_table.shape
    n_col = heads * t_new
    q4 = q_new.reshape(dec_batch, t_new, heads, HEAD_DIM)
    qbd = jnp.einsum('bthe,hg->bhegt', q4, jnp.eye(heads, dtype=F32)).reshape(dec_batch, d, n_col)
    bias_cols = jnp.repeat(bias.astype(F32), t_new).reshape(1, n_col)
    pad = ((0, 0), (0, PAGE_SIZE - t_new), (0, 0))
    kn = jnp.pad(k_new.reshape(dec_batch, t_new, d), pad)
    vn = jnp.pad(v_new.reshape(dec_batch, t_new, d), pad)
    steps = n_pages // pages_per_step

    def pool_rows(c):
        return c.reshape(c.shape[0], c.shape[1], PAGE_SIZE * heads, HEAD_DIM)

    def page_spec(p):
        return pl.BlockSpec(
            (None, None, PAGE_SIZE * heads, HEAD_DIM),
            lambda b, s, pt, p=p: (layer, pt[b, n_pages - 1 - (s * pages_per_step + p)], 0, 0))

    grid_spec = pltpu.PrefetchScalarGridSpec(
        num_scalar_prefetch=1,
        grid=(dec_batch, steps),
        in_specs=[
            pl.BlockSpec((None, d, n_col), lambda b, s, pt: (b, 0, 0)),
            pl.BlockSpec((1, n_col), lambda b, s, pt: (0, 0)),
            pl.BlockSpec((None, PAGE_SIZE, d), lambda b, s, pt: (b, 0, 0)),
            pl.BlockSpec((None, PAGE_SIZE, d), lambda b, s, pt: (b, 0, 0)),
        ] + [page_spec(p) for p in range(pages_per_step)] * 2,
        out_specs=pl.BlockSpec((t_new, d), lambda b, s, pt: (b, 0)),
        scratch_shapes=[pltpu.VMEM((1, n_col), F32)],
    )
    return pl.pallas_call(
        functools.partial(_sb_sample_kernel, pages_per_step=pages_per_step, t_new=t_new, heads=heads),
        grid_spec=grid_spec,
        out_shape=jax.ShapeDtypeStruct((m, d), F32),
        compiler_params=_params("arbitrary", "arbitrary"),
        name="sb_sample_attention",
    )(page_table, qbd, bias_cols, kn, vn, *([pool_rows(cache_k)] * pages_per_step),
      *([pool_rows(cache_v)] * pages_per_step))


def _head_rows_kernel(*refs, heads):
    xs, o_ref = refs[:-1], refs[-1]
    layer = pl.program_id(0)
    tm = xs[0].shape[0]
    for l, x_ref in enumerate(xs):
        @pl.when(layer == l)
        def _(x_ref=x_ref):
            for h in range(heads):
                o_ref[pl.ds(h, tm, stride=heads), :] = x_ref[:, h * HEAD_DIM:(h + 1) * HEAD_DIM]


def stack_head_rows(xs, seg, batch, seq, tm=512):
    _, m, d = xs[0].shape
    heads = d // HEAD_DIM
    nt = m // tm

    def in_spec(l):
        return pl.BlockSpec((None, tm, d),
                            lambda lay, i: (seg, jnp.where(lay == l, i, jnp.where(lay < l, 0, nt - 1)), 0))

    out = pl.pallas_call(
        functools.partial(_head_rows_kernel, heads=heads),
        grid=(len(xs), nt),
        in_specs=[in_spec(l) for l in range(len(xs))],
        out_specs=pl.BlockSpec((None, tm * heads, HEAD_DIM), lambda lay, i: (lay, i, 0)),
        out_shape=jax.ShapeDtypeStruct((len(xs), m * heads, HEAD_DIM), xs[0].dtype),
        compiler_params=_params("arbitrary", "arbitrary"),
        name="stack_head_rows",
    )(*xs)
    return out.reshape(len(xs), batch, seq, heads, HEAD_DIM)


def _alibi_slopes(heads):
    n = len(SWA_DILATIONS) * heads
    s = 2.0 ** (-8.0 * np.arange(1, n + 1) / n)
    return jnp.asarray(s.reshape(len(SWA_DILATIONS), heads), dtype=F32)


def _lane_table(per_head, width):
    heads, n = per_head.shape
    tab = jnp.pad(per_head, ((0, 0), (0, SUBLANES - n)))
    return jnp.broadcast_to(tab[:, :, None], (heads, SUBLANES, width))


def _last_rows(t, n):
    seq = t.shape[1]
    if seq >= n:
        return t[:, seq - n:]
    return jnp.pad(t, ((0, 0), (n - seq, 0), (0, 0), (0, 0)))


def kernel(x_prompt, x_sample, cache_swa_k0, cache_swa_v0, cache_swa_k1, cache_swa_v1, cache_swa_k2, cache_swa_v2, cache_sb_k, cache_sb_v, page_table, c_prompt, c_sample, w_ada, b_ada, norm_mix_g, norm_ffn_g, w_qkv_swa, w_o_swa, w_qkv_sb, w_o_sb, sb_bias, w_gate, w_up, w_down, final_norm_g):
    batch, seq, d = x_prompt.shape
    dec_batch, t_new, _ = x_sample.shape
    depth = w_ada.shape[0]
    heads = d // HEAD_DIM
    n_pages = page_table.shape[1]
    past_len = n_pages * cache_sb_k.shape[2]
    mp_rows, ms_rows = batch * seq, dec_batch * t_new
    tm_p, tm_s = 1024, ms_rows
    sb_tq, sb_hps = 256, 4

    slopes_tab = _lane_table(_alibi_slopes(heads).T, LANES)
    swa_caches = (cache_swa_k0, cache_swa_v0, cache_swa_k1, cache_swa_v1, cache_swa_k2, cache_swa_v2)
    swa_s = [swa_shift_window(c, t_new) for c in swa_caches]

    n_c = batch + dec_batch
    c_rows = -(-n_c // SUBLANES) * SUBLANES
    c_all = jnp.pad(jnp.concatenate([c_prompt, c_sample], axis=0), ((0, c_rows - n_c), (0, 0)))
    mods = ada_all(c_all, w_ada, b_ada)

    xp = x_prompt.reshape(mp_rows, d)
    xs = x_sample.reshape(ms_rows, d)
    swa_qkv_p, sb_qkv_p = [], []
    sb_s = [[], []]

    for i in range(depth):
        mod_i = mods[i].reshape(c_rows, 6, d)
        mp = [mod_i[:batch, j][:, None, :] for j in range(6)]
        ms = [jnp.repeat(mod_i[batch:n_c, j], t_new, axis=0)[None] for j in range(6)]
        hp = norm_mod(xp, norm_mix_g[i], mp[0], mp[1], tm_p)
        hs = norm_mod(xs, norm_mix_g[i], ms[0], ms[1], tm_s)
        if i % 2 == 0:
            a = i // 2
            qkv_p = project_segments(hp, w_qkv_swa, a, d, tm_p, 1024)
            qkv_s = project_segments(hs, w_qkv_swa, a, d, tm_s, 1024)
            op = swa_prompt_attention(qkv_p, slopes_tab, batch, seq)
            os_ = swa_sample_attention(qkv_s, swa_caches, a, slopes_tab, dec_batch, t_new, past_len)
            news = [qkv_s[3 * g + 1 + c].reshape(dec_batch, t_new, heads, HEAD_DIM)
                    for g in range(len(SWA_WINDOWS)) for c in range(2)]
            swa_s = swa_write_new_rows(news, swa_s, a, t_new)
            swa_qkv_p.append(qkv_p)
            w_o, la = w_o_swa, a
        else:
            b = i // 2
            qkv_p = project_segments(hp, w_qkv_sb, b, d, tm_p, 1024, scaled_seg0=HEAD_DIM ** -0.5)
            qkv_s = project_segments(hs, w_qkv_sb, b, d, tm_s, 1024, scaled_seg0=HEAD_DIM ** -0.5)
            op = sb_prompt_attention(qkv_p, sb_bias[b], batch, seq, sb_tq, sb_hps)
            os_ = sb_sample_attention(qkv_s[0], qkv_s[1], qkv_s[2], cache_sb_k, cache_sb_v, b,
                                      page_table, sb_bias[b], t_new)
            sb_qkv_p.append(qkv_p)
            for c in range(2):
                sb_s[c].append(qkv_s[1 + c].reshape(dec_batch, t_new, heads, HEAD_DIM))
            w_o, la = w_o_sb, b
        xp = project_residual(op, w_o, la, xp, mp[2], tm_p, 1024)
        xs = project_residual(os_, w_o, la, xs, ms[2], tm_s, 1024)
        hp = norm_mod(xp, norm_ffn_g[i], mp[3], mp[4], tm_p)
        hs = norm_mod(xs, norm_ffn_g[i], ms[3], ms[4], tm_s)
        ap = project_swiglu(hp, w_gate, w_up, i, tm_p, 512)
        as_ = project_swiglu(hs, w_gate, w_up, i, tm_s, 512)
        xp = project_residual(ap, w_down, i, xp, mp[5], tm_p, 256)
        xs = project_residual(as_, w_down, i, xs, ms[5], tm_s, 256)

    y_prompt = final_norm(xp, final_norm_g, tm_p).reshape(batch, seq, d)
    y_sample = final_norm(xs, final_norm_g, tm_s).reshape(dec_batch, t_new, d)
    outs = [y_prompt, y_sample]
    for g, win in enumerate(SWA_WINDOWS):
        for c in range(2):
            seg = 3 * g + 1 + c
            if win == seq:
                outs.append(stack_head_rows(swa_qkv_p, seg, batch, seq))
            else:
                outs.append(jnp.stack([_last_rows(q[seg].reshape(batch, seq, heads, HEAD_DIM), win)
                                       for q in swa_qkv_p]))
    outs += [stack_head_rows(sb_qkv_p, 1 + c, batch, seq) for c in range(2)]
    outs += list(swa_s)
    outs += [jnp.stack(t) for t in sb_s]
    return tuple(outs)
```

```python
import functools
import math

import numpy as np
import jax
import jax.numpy as jnp
from jax import lax
from jax.experimental import pallas as pl
from jax.experimental.pallas import tpu as pltpu

F32 = jnp.float32
BF16 = jnp.bfloat16

HEAD_DIM = 128
SWA_WINDOWS = (128, 512, 2048)
SWA_DILATIONS = (1, 4, 16)
SWA_SPAN = 128
Q_BLOCK = 128
PAGE_SIZE = 128
EPS = 1e-6
MASK_VALUE = -1e30
LANES = 128
SUBLANES = 8
VMEM_LIMIT_BYTES = 56 * 1024 * 1024

NT_DIMS = (((1,), (1,)), ((), ()))
TN_DIMS = (((0,), (0,)), ((), ()))


def _params(*sem):
    return pltpu.CompilerParams(dimension_semantics=sem, vmem_limit_bytes=VMEM_LIMIT_BYTES)


def _dot(a, b, dims=None):
    if dims is None:
        return jnp.dot(a, b, preferred_element_type=F32)
    return lax.dot_general(a, b, dims, preferred_element_type=F32)


def _softplus(z):
    return jnp.maximum(z, 0.0) + jnp.log(1.0 + jnp.exp(-jnp.abs(z)))


def _split_bf16(x):
    hi = x.astype(BF16)
    lo = (x - hi.astype(F32)).astype(BF16)
    return hi, lo


def _ada_kernel(c_ref, w_ref, b_ref, o_ref):
    c = c_ref[...]
    a = (c / (1.0 + jnp.exp(-c))).astype(BF16)
    o_ref[...] = _dot(a, w_ref[...].astype(BF16)) + b_ref[...]


def ada_all(c_all, w_ada, b_ada, tn=1024):
    depth, d, n = w_ada.shape
    r = c_all.shape[0]
    return pl.pallas_call(
        _ada_kernel,
        grid=(depth, n // tn),
        in_specs=[
            pl.BlockSpec((r, d), lambda l, j: (0, 0)),
            pl.BlockSpec((None, d, tn), lambda l, j: (l, 0, j)),
            pl.BlockSpec((None, 1, tn), lambda l, j: (l, 0, j)),
        ],
        out_specs=pl.BlockSpec((None, r, tn), lambda l, j: (l, 0, j)),
        out_shape=jax.ShapeDtypeStruct((depth, r, n), F32),
        compiler_params=_params("arbitrary", "arbitrary"),
        name="ada_mod",
    )(c_all, w_ada, b_ada.reshape(depth, 1, n))


def _rms(x):
    return x * lax.rsqrt(jnp.mean(x * x, axis=-1, keepdims=True) + EPS)


def _norm_mod_kernel(x_ref, g_ref, sh_ref, sc_ref, o_ref):
    y = _rms(x_ref[...]) * g_ref[...]
    o_ref[...] = (y * (1.0 + sc_ref[...]) + sh_ref[...]).astype(o_ref.dtype)


def _norm_kernel(x_ref, g_ref, o_ref):
    o_ref[...] = (_rms(x_ref[...]) * g_ref[...]).astype(o_ref.dtype)


def _mod_spec(mod, tiles_per_group, width, two_d):
    r = mod.shape[1]
    if two_d:
        return pl.BlockSpec((None, r, width), lambda i, j: (i // tiles_per_group, 0, j))
    return pl.BlockSpec((None, r, width), lambda i: (i // tiles_per_group, 0, 0))


def norm_mod(x, g, shift, scale, tm):
    m, d = x.shape
    tpg = (m // tm) // shift.shape[0]
    return pl.pallas_call(
        _norm_mod_kernel,
        grid=(m // tm,),
        in_specs=[
            pl.BlockSpec((tm, d), lambda i: (i, 0)),
            pl.BlockSpec((1, d), lambda i: (0, 0)),
            _mod_spec(shift, tpg, d, False),
            _mod_spec(scale, tpg, d, False),
        ],
        out_specs=pl.BlockSpec((tm, d), lambda i: (i, 0)),
        out_shape=jax.ShapeDtypeStruct((m, d), BF16),
        compiler_params=_params("arbitrary"),
        name="norm_mod",
    )(x, g.reshape(1, d), shift, scale)


def final_norm(x, g, tm):
    m, d = x.shape
    return pl.pallas_call(
        _norm_kernel,
        grid=(m // tm,),
        in_specs=[pl.BlockSpec((tm, d), lambda i: (i, 0)), pl.BlockSpec((1, d), lambda i: (0, 0))],
        out_specs=pl.BlockSpec((tm, d), lambda i: (i, 0)),
        out_shape=jax.ShapeDtypeStruct((m, d), F32),
        compiler_params=_params("arbitrary"),
        name="final_norm",
    )(x, g.reshape(1, d))


def _proj_kernel(a_ref, w_ref, o_ref, *, scaled_tiles, scale):
    acc = _dot(a_ref[...].astype(BF16), w_ref[...].astype(BF16))
    if scaled_tiles:
        acc = acc * jnp.where(pl.program_id(1) < scaled_tiles, scale, 1.0).astype(F32)
    o_ref[...] = acc


def project_segments(a, w, layer, seg, tm, tn, scaled_seg0=None):
    m, k = a.shape
    n = w.shape[2]
    tps = seg // tn
    kern = functools.partial(
        _proj_kernel,
        scaled_tiles=tps if scaled_seg0 is not None else 0,
        scale=scaled_seg0 if scaled_seg0 is not None else 1.0,
    )
    return pl.pallas_call(
        kern,
        grid=(m // tm, n // tn),
        in_specs=[
            pl.BlockSpec((tm, k), lambda i, j: (i, 0)),
            pl.BlockSpec((None, k, tn), lambda i, j: (layer, 0, j)),
        ],
        out_specs=pl.BlockSpec((None, tm, tn), lambda i, j: (j // tps, i, j % tps)),
        out_shape=jax.ShapeDtypeStruct((n // seg, m, seg), F32),
        compiler_params=_params("arbitrary", "arbitrary"),
        name="project_segments",
    )(a, w)


def _residual_kernel(a_ref, w_ref, x_ref, g_ref, o_ref):
    acc = _dot(a_ref[...].astype(BF16), w_ref[...].astype(BF16))
    o_ref[...] = x_ref[...] + g_ref[...] * acc


def project_residual(a, w, layer, x, gate, tm, tn):
    m, k = a.shape
    n = w.shape[2]
    tpg = (m // tm) // gate.shape[0]
    return pl.pallas_call(
        _residual_kernel,
        grid=(m // tm, n // tn),
        in_specs=[
            pl.BlockSpec((tm, k), lambda i, j: (i, 0)),
            pl.BlockSpec((None, k, tn), lambda i, j: (layer, 0, j)),
            pl.BlockSpec((tm, tn), lambda i, j: (i, j)),
            _mod_spec(gate, tpg, tn, True),
        ],
        out_specs=pl.BlockSpec((tm, tn), lambda i, j: (i, j)),
        out_shape=jax.ShapeDtypeStruct((m, n), F32),
        compiler_params=_params("arbitrary", "arbitrary"),
        name="project_residual",
    )(a, w, x, gate)


def _swiglu_kernel(a_ref, wg_ref, wu_ref, o_ref):
    a = a_ref[...].astype(BF16)
    g = _dot(a, wg_ref[...].astype(BF16))
    u = _dot(a, wu_ref[...].astype(BF16))
    o_ref[...] = ((g / (1.0 + jnp.exp(-g))) * u).astype(o_ref.dtype)


def project_swiglu(a, w_gate, w_up, layer, tm, tn):
    m, k = a.shape
    n = w_gate.shape[2]
    wspec = pl.BlockSpec((None, k, tn), lambda i, j: (layer, 0, j))
    return pl.pallas_call(
        _swiglu_kernel,
        grid=(m // tm, n // tn),
        in_specs=[pl.BlockSpec((tm, k), lambda i, j: (i, 0)), wspec, wspec],
        out_specs=pl.BlockSpec((tm, tn), lambda i, j: (i, j)),
        out_shape=jax.ShapeDtypeStruct((m, n), BF16),
        compiler_params=_params("arbitrary", "arbitrary"),
        name="project_swiglu",
    )(a, w_gate, w_up)


def _rows(start, dil):
    return pl.ds(start, Q_BLOCK) if dil == 1 else pl.ds(start, Q_BLOCK, stride=dil)


def _pick_unroll(n):
    for u in (4, 5, 3, 2):
        if n % u == 0:
            return u
    return 1


def _swa_prompt_kernel(sl_ref, q0, k0, v0, q1, k1, v1, q2, k2, v2, o_ref,
                       og_scr, lse_scr, bias_scr, *, seq):
    qkv = ((q0, k0, v0), (q1, k1, v1), (q2, k2, v2))
    scale = HEAD_DIM ** -0.5
    qi = lax.broadcasted_iota(jnp.int32, (Q_BLOCK, Q_BLOCK), 0)
    kj = lax.broadcasted_iota(jnp.int32, (Q_BLOCK, Q_BLOCK), 1)

    for g, dil in enumerate(SWA_DILATIONS):
        slope = sl_ref[g:g + 1, :]
        d_cur = qi - kj
        d_prev = Q_BLOCK + qi - kj
        bias_scr[g, 0] = jnp.where(d_cur >= 0, -(slope * (d_cur * dil).astype(F32)), MASK_VALUE)
        bias_scr[g, 1] = jnp.where(d_prev <= SWA_SPAN, -(slope * (d_prev * dil).astype(F32)), MASK_VALUE)

    for g, dil in enumerate(SWA_DILATIONS):
        q_ref, k_ref, v_ref = qkv[g]
        blocks_per_res = seq // (dil * Q_BLOCK)

        def blocks(starts, with_prev, g=g, dil=dil, q_ref=q_ref, k_ref=k_ref, v_ref=v_ref):
            n = len(starts)
            rows = [_rows(s, dil) for s in starts]
            q = [q_ref[r, :].astype(BF16) for r in rows]
            kc = [k_ref[r, :].astype(BF16) for r in rows]
            vc = [v_ref[r, :].astype(BF16) for r in rows]
            if with_prev:
                prow = [_rows(s - Q_BLOCK * dil, dil) for s in starts]
                kp = [k_ref[r, :].astype(BF16) for r in prow]
                vp = [v_ref[r, :].astype(BF16) for r in prow]
            s_c = [_dot(q[i], kc[i], NT_DIMS) * scale + bias_scr[g, 0] for i in range(n)]
            m = [jnp.max(s, axis=-1, keepdims=True) for s in s_c]
            if with_prev:
                s_p = [_dot(q[i], kp[i], NT_DIMS) * scale + bias_scr[g, 1] for i in range(n)]
                m = [jnp.maximum(m[i], jnp.max(s_p[i], axis=-1, keepdims=True)) for i in range(n)]
            p_c = [jnp.exp(s_c[i] - m[i]) for i in range(n)]
            den = [jnp.sum(p, axis=-1, keepdims=True) for p in p_c]
            if with_prev:
                p_p = [jnp.exp(s_p[i] - m[i]) for i in range(n)]
                den = [den[i] + jnp.sum(p_p[i], axis=-1, keepdims=True) for i in range(n)]
            inv = [1.0 / d_ for d_ in den]
            o = [_dot((p_c[i] * inv[i]).astype(BF16), vc[i]) for i in range(n)]
            if with_prev:
                o = [o[i] + _dot((p_p[i] * inv[i]).astype(BF16), vp[i]) for i in range(n)]
            lse = [jnp.broadcast_to(m[i] + jnp.log(den[i]), (Q_BLOCK, LANES)) for i in range(n)]
            for i in range(n):
                og_scr[g, rows[i], :] = o[i]
                lse_scr[g, rows[i], :] = lse[i]

        def run(n_blocks, start_of, with_prev, blocks=blocks):
            unroll = _pick_unroll(n_blocks)

            def body(it, carry):
                blocks([start_of(it * unroll + u) for u in range(unroll)], with_prev)
                return carry

            lax.fori_loop(0, n_blocks // unroll, body, 0)

        run(dil, lambda r: r, False)
        if blocks_per_res > 1:
            per = blocks_per_res - 1
            run(dil * per, lambda it, per=per, dil=dil: it // per + (it % per + 1) * (Q_BLOCK * dil), True)

    chunk = 256

    def merge_body(c, carry):
        rows = pl.ds(pl.multiple_of(c * chunk, chunk), chunk)
        l0, l1, l2 = lse_scr[0, rows, :], lse_scr[1, rows, :], lse_scr[2, rows, :]
        mx = jnp.maximum(jnp.maximum(l0, l1), l2)
        w0, w1, w2 = jnp.exp(l0 - mx), jnp.exp(l1 - mx), jnp.exp(l2 - mx)
        inv = 1.0 / (w0 + w1 + w2)
        o = (w0 * inv) * og_scr[0, rows, :] + (w1 * inv) * og_scr[1, rows, :] + (w2 * inv) * og_scr[2, rows, :]
        o_ref[rows, :] = o.astype(o_ref.dtype)
        return carry

    lax.fori_loop(0, seq // chunk, merge_body, 0)


def swa_prompt_attention(qkv, slopes_tab, batch, seq):
    _, m, d = qkv.shape
    heads = d // HEAD_DIM
    specs = [pl.BlockSpec((None, SUBLANES, LANES), lambda b, h: (h, 0, 0))]
    for s in range(9):
        specs.append(pl.BlockSpec((None, seq, HEAD_DIM), lambda b, h, s=s: (s, b, h)))
    return pl.pallas_call(
        functools.partial(_swa_prompt_kernel, seq=seq),
        grid=(batch, heads),
        in_specs=specs,
        out_specs=pl.BlockSpec((seq, HEAD_DIM), lambda b, h: (b, h)),
        out_shape=jax.ShapeDtypeStruct((m, d), BF16),
        scratch_shapes=[
            pltpu.VMEM((3, seq, HEAD_DIM), F32),
            pltpu.VMEM((3, seq, LANES), F32),
            pltpu.VMEM((3, 2, Q_BLOCK, Q_BLOCK), F32),
        ],
        compiler_params=_params("arbitrary", "arbitrary"),
        name="swa_prompt_attention",
    )(slopes_tab, *([qkv] * 9))


SHIFT_BLOCK_ROWS = 8192


def _shift_rows_kernel(x_ref, o_ref, *, shift):
    last = pl.program_id(0) == pl.num_programs(0) - 1

    @pl.when(jnp.logical_not(last))
    def _():
        o_ref[...] = x_ref[...]

    @pl.when(last)
    def _():
        rows = o_ref.shape[0]
        o_ref[0:rows - shift, :] = x_ref[shift:rows, :]
        o_ref[rows - shift:rows, :] = x_ref[rows - shift:rows, :]


def swa_shift_window(cache, t_new):
    layers, dec_batch, win, heads, hd = cache.shape
    total = layers * dec_batch * win * heads
    shift = t_new * heads
    rows = min(SHIFT_BLOCK_ROWS, total)
    assert rows % shift == 0 and total % rows == 0
    per_block, last_start = rows // shift, (total - rows) // shift
    out = pl.pallas_call(
        functools.partial(_shift_rows_kernel, shift=shift),
        grid=(total // rows,),
        in_specs=[pl.BlockSpec((pl.Element(rows), pl.Element(hd)),
                               lambda i: (jnp.minimum(i * per_block + 1, last_start) * shift, 0))],
        out_specs=pl.BlockSpec((rows, hd), lambda i: (i, 0)),
        out_shape=jax.ShapeDtypeStruct((total, hd), cache.dtype),
        compiler_params=_params("arbitrary"),
        name="swa_shift_window",
    )(cache.reshape(total, hd))
    return out.reshape(cache.shape)


def _new_rows_kernel(*refs, n):
    news, outs = refs[:n], refs[2 * n:]
    for c in range(n):
        outs[c][...] = news[c][...]


def swa_write_new_rows(news, bufs, layer, t_new):
    n = len(bufs)
    dec_batch, _, heads, hd = news[0].shape
    specs = [pl.BlockSpec((None, t_new, heads, hd), lambda b: (b, 0, 0, 0))] * n
    specs += [pl.BlockSpec(memory_space=pl.ANY)] * n
    out_specs = [pl.BlockSpec((None, None, t_new, heads, hd),
                              lambda b, last=buf.shape[2] // t_new - 1: (layer, b, last, 0, 0)) for buf in bufs]
    return pl.pallas_call(
        functools.partial(_new_rows_kernel, n=n),
        grid=(dec_batch,),
        in_specs=specs,
        out_specs=out_specs,
        out_shape=[jax.ShapeDtypeStruct(b.shape, b.dtype) for b in bufs],
        input_output_aliases={n + c: c for c in range(n)},
        compiler_params=_params("arbitrary"),
        name="swa_write_new_rows",
    )(*news, *bufs)


def _swa_sample_kernel(sl_ref, r_ref, k0, v0, k1, v1, k2, v2, o_ref, m_scr, l_scr, acc_scr,
                       *, t_new, heads, past_len):
    step = pl.program_id(1)
    scale = HEAD_DIM ** -0.5
    chunks = k2.shape[0]
    near = ((k0, v0), (k1, v1))

    def update(state, q_row, k, v, slope, dist, valid):
        m, l, acc = state
        s = jnp.sum(k * q_row, axis=-1, keepdims=True) * scale - slope * dist.astype(F32)
        s = jnp.where(valid, s, MASK_VALUE)
        m_new = jnp.maximum(m, jnp.max(s, axis=0, keepdims=True))
        p = jnp.exp(s - m_new)
        alpha = jnp.exp(m - m_new)
        l = l * alpha + jnp.sum(p, axis=0, keepdims=True)
        acc = acc * alpha + jnp.sum(p * v, axis=0, keepdims=True)
        return m_new, l, acc

    def far_piece(state, h, t):
        win, dil = SWA_WINDOWS[2], SWA_DILATIONS[2]
        i = step * chunks + lax.broadcasted_iota(jnp.int32, (chunks, 1), 0)
        dist = win - dil * i
        valid = (dist > 0) & (dist <= SWA_SPAN * dil) & ((past_len - win) + t + dil * i >= 0)
        q_row = r_ref[2, 0, h, t:t + 1, :]
        slope = sl_ref[h, 2:3, 0:1]
        return update(state, q_row, k2[:, t * heads + h, :], v2[:, t * heads + h, :], slope, dist, valid)

    def first_body(h, carry):
        slabs = {}
        for g, (k_ref, v_ref) in enumerate(near):
            dil = SWA_DILATIONS[g]
            for first in range(min(dil, t_new)):
                rows = pl.ds(first * heads + h, SWA_SPAN, stride=dil * heads)
                slabs[g, first] = (k_ref[rows, :], v_ref[rows, :])
        for t in range(t_new):
            state = (jnp.full((1, 1), MASK_VALUE, F32), jnp.zeros((1, 1), F32), jnp.zeros((1, HEAD_DIM), F32))
            u = lax.broadcasted_iota(jnp.int32, (t_new, 1), 0)
            for g, dil in enumerate(SWA_DILATIONS):
                dist = t - u
                valid = (dist >= 0) & ((dist & (dil - 1)) == 0)
                state = update(state, r_ref[g, 0, h, t:t + 1, :], r_ref[g, 1, h], r_ref[g, 2, h],
                               sl_ref[h, g:g + 1, 0:1], dist, valid)
            i = lax.broadcasted_iota(jnp.int32, (SWA_SPAN, 1), 0)
            for g in range(len(near)):
                win, dil = SWA_WINDOWS[g], SWA_DILATIONS[g]
                first = t % dil
                tok = first + dil * i
                dist = win + t - tok
                valid = (dist > 0) & (dist <= SWA_SPAN * dil) & ((past_len - win) + tok >= 0)
                state = update(state, r_ref[g, 0, h, t:t + 1, :], *slabs[g, first],
                               sl_ref[h, g:g + 1, 0:1], dist, valid)
            m, l, acc = far_piece(state, h, t)
            m_scr[h, t:t + 1, :] = jnp.broadcast_to(m, (1, LANES))
            l_scr[h, t:t + 1, :] = jnp.broadcast_to(l, (1, LANES))
            acc_scr[h, t:t + 1, :] = acc
        return carry

    def last_body(h, carry):
        for t in range(t_new):
            state = (m_scr[h, t:t + 1, 0:1], l_scr[h, t:t + 1, 0:1], acc_scr[h, t:t + 1, :])
            m, l, acc = far_piece(state, h, t)
            o_ref[h, t:t + 1, :] = acc / l
        return carry

    @pl.when(step == 0)
    def _():
        lax.fori_loop(0, heads, first_body, 0)

    @pl.when(step == 1)
    def _():
        lax.fori_loop(0, heads, last_body, 0)


def swa_sample_attention(qkv_new, caches, layer, slopes_tab, dec_batch, t_new, past_len):
    _, m, d = qkv_new.shape
    heads = d // HEAD_DIM
    far_dil, far_win = SWA_DILATIONS[2], SWA_WINDOWS[2]
    assert 2 * t_new <= far_dil and SWA_DILATIONS[0] == 1 and t_new <= 2 * SWA_DILATIONS[1]
    r = qkv_new.reshape(3, 3, dec_batch, t_new, heads, HEAD_DIM).transpose(0, 1, 2, 4, 3, 5)
    near = [c.reshape(c.shape[0], dec_batch, c.shape[2] * heads, HEAD_DIM) for c in caches[:4]]
    n_chunks = far_win // far_dil
    far = [c.reshape(c.shape[0], dec_batch, n_chunks, far_dil * heads, HEAD_DIM) for c in caches[4:]]
    steps = 2
    specs = [
        pl.BlockSpec((heads, SUBLANES, LANES), lambda b, s: (0, 0, 0)),
        pl.BlockSpec((3, 3, None, heads, t_new, HEAD_DIM), lambda b, s: (0, 0, b, 0, 0, 0)),
    ]
    for c in near:
        specs.append(pl.BlockSpec((None, None, c.shape[2], HEAD_DIM), lambda b, s: (layer, b, 0, 0)))
    for c in far:
        specs.append(pl.BlockSpec((None, None, n_chunks // steps, far_dil * heads // 2, HEAD_DIM),
                                  lambda b, s: (layer, b, s, 0, 0)))
    o = pl.pallas_call(
        functools.partial(_swa_sample_kernel, t_new=t_new, heads=heads, past_len=past_len),
        grid=(dec_batch, steps),
        in_specs=specs,
        out_specs=pl.BlockSpec((None, heads, t_new, HEAD_DIM), lambda b, s: (b, 0, 0, 0)),
        out_shape=jax.ShapeDtypeStruct((dec_batch, heads, t_new, HEAD_DIM), F32),
        scratch_shapes=[pltpu.VMEM((heads, t_new, LANES), F32), pltpu.VMEM((heads, t_new, LANES), F32),
                        pltpu.VMEM((heads, t_new, HEAD_DIM), F32)],
        compiler_params=_params("arbitrary", "arbitrary"),
        name="swa_sample_attention",
    )(slopes_tab, r, *near, *far)
    return o.transpose(0, 2, 1, 3).reshape(m, d)


def _sb_prompt_kernel(b_ref, q_ref, k_ref, v_ref, o_ref, *, tq, hps):
    qb = pl.program_id(2)
    row = lax.broadcasted_iota(jnp.int32, (tq, tq), 0)
    col = lax.broadcasted_iota(jnp.int32, (tq, tq), 1)
    newer_mat = jnp.where(row > col, 1.0, 0.0).astype(BF16)
    causal = col < row

    def key_block(kb, carry, acc, masked):
        heads = range(hps)
        cols = [slice(h * HEAD_DIM, (h + 1) * HEAD_DIM) for h in heads]
        ks = pl.ds(pl.multiple_of(kb * tq, tq), tq)
        q = [q_ref[:, c].astype(BF16) for c in cols]
        k = [k_ref[ks, c].astype(BF16) for c in cols]
        v = [v_ref[ks, c].astype(BF16) for c in cols]
        z = [_dot(q[h], k[h], NT_DIMS) + b_ref[h:h + 1, :] for h in heads]
        log_keep = [-_softplus(z[h]) for h in heads]
        log_beta = [z[h] + log_keep[h] for h in heads]
        if masked:
            log_keep = [jnp.where(causal, lk, 0.0) for lk in log_keep]
        split = [_split_bf16(lk) for lk in log_keep]
        newer = [_dot(split[h][0], newer_mat) + _dot(split[h][1], newer_mat) + carry[h] for h in heads]
        a = [jnp.exp(log_beta[h] + newer[h]) for h in heads]
        if masked:
            a = [jnp.where(causal, x, 0.0) for x in a]
        acc = tuple(acc[h] + _dot(a[h].astype(BF16), v[h]) for h in heads)
        carry = tuple(carry[h] + jnp.sum(log_keep[h], axis=-1, keepdims=True) for h in heads)
        return carry, acc

    carry0 = tuple(jnp.zeros((tq, 1), F32) for _ in range(hps))
    acc0 = tuple(jnp.zeros((tq, HEAD_DIM), F32) for _ in range(hps))
    state = key_block(qb, carry0, acc0, True)
    state = lax.fori_loop(0, qb, lambda it, st: key_block(qb - 1 - it, st[0], st[1], False), state)
    for h in range(hps):
        o_ref[:, h * HEAD_DIM:(h + 1) * HEAD_DIM] = state[1][h].astype(o_ref.dtype)


def sb_prompt_attention(qkv, bias, batch, seq, tq, hps):
    _, m, d = qkv.shape
    heads = d // HEAD_DIM
    nq = seq // tq
    width = hps * HEAD_DIM
    bias_tab = _lane_table(bias.astype(F32).reshape(heads // hps, hps), tq)
    return pl.pallas_call(
        functools.partial(_sb_prompt_kernel, tq=tq, hps=hps),
        grid=(batch, heads // hps, nq),
        in_specs=[
            pl.BlockSpec((None, SUBLANES, tq), lambda b, h, i: (h, 0, 0)),
            pl.BlockSpec((None, tq, width), lambda b, h, i: (0, b * nq + i, h)),
            pl.BlockSpec((None, seq, width), lambda b, h, i: (1, b, h)),
            pl.BlockSpec((None, seq, width), lambda b, h, i: (2, b, h)),
        ],
        out_specs=pl.BlockSpec((tq, width), lambda b, h, i: (b * nq + i, h)),
        out_shape=jax.ShapeDtypeStruct((m, d), BF16),
        compiler_params=_params("arbitrary", "arbitrary", "arbitrary"),
        name="sb_prompt_attention",
    )(bias_tab, qkv, qkv, qkv)


def _sb_sample_kernel(pt_ref, qbd_ref, bias_ref, kn_ref, vn_ref, *refs, pages_per_step, t_new, heads):
    del pt_ref
    k_refs = refs[:pages_per_step]
    v_refs = refs[pages_per_step:2 * pages_per_step]
    o_ref = refs[2 * pages_per_step]
    carry_scr = refs[2 * pages_per_step + 1]
    step = pl.program_id(1)
    qbd = qbd_ref[...].astype(BF16)
    bias = bias_ref[...]
    n_col = heads * t_new
    key = lax.broadcasted_iota(jnp.int32, (PAGE_SIZE, n_col), 0)
    tok = lax.broadcasted_iota(jnp.int32, (PAGE_SIZE, n_col), 1) % t_new
    row = lax.broadcasted_iota(jnp.int32, (PAGE_SIZE, PAGE_SIZE), 0)
    col = lax.broadcasted_iota(jnp.int32, (PAGE_SIZE, PAGE_SIZE), 1)
    newer_mat = jnp.where(col > row, 1.0, 0.0).astype(BF16)

    def pages(loaders, masked):
        n = len(loaders)
        k = [jnp.concatenate([hk(h) for h in range(heads)], axis=1).astype(BF16) for hk, _ in loaders]
        z = [_dot(k[i], qbd) + bias for i in range(n)]
        log_keep = [-_softplus(x) for x in z]
        log_beta = [z[i] + log_keep[i] for i in range(n)]
        if masked:
            causal = key < tok
            log_keep = [jnp.where(causal, lk, 0.0) for lk in log_keep]
        split = [_split_bf16(lk) for lk in log_keep]
        within = [_dot(newer_mat, hi) + _dot(newer_mat, lo) for hi, lo in split]
        carry = carry_scr[...]
        a_t = []
        for i in range(n):
            a = jnp.exp(log_beta[i] + (within[i] + carry))
            if masked:
                a = jnp.where(causal, a, 0.0)
            a_t.append(a.T)
            carry = carry + jnp.sum(log_keep[i], axis=0, keepdims=True)
        carry_scr[...] = carry
        for h in range(heads):
            rows = slice(h * t_new, (h + 1) * t_new)
            o = _dot(a_t[0][rows, :].astype(BF16), loaders[0][1](h).astype(BF16))
            for i in range(1, n):
                o = o + _dot(a_t[i][rows, :].astype(BF16), loaders[i][1](h).astype(BF16))
            o_ref[:, h * HEAD_DIM:(h + 1) * HEAD_DIM] += o

    def head_rows(ref):
        return lambda h: ref[pl.ds(h, PAGE_SIZE, stride=heads), :]

    @pl.when(step == 0)
    def _():
        o_ref[...] = jnp.zeros_like(o_ref)
        carry_scr[...] = jnp.zeros_like(carry_scr)
        pages([(lambda h: kn_ref[:, h * HEAD_DIM:(h + 1) * HEAD_DIM],
                lambda h: vn_ref[:, h * HEAD_DIM:(h + 1) * HEAD_DIM])], True)

    pages([(head_rows(k_refs[p]), head_rows(v_refs[p])) for p in range(pages_per_step)], False)


def sb_sample_attention(q_new, k_new, v_new, cache_k, cache_v, layer, page_table, bias, t_new, pages_per_step=4):
    m, d = q_new.shape
    heads = d // HEAD_DIM
    dec_batch, n_pages = page_table.shape
    n_col = heads * t_new
    q4 = q_new.reshape(dec_batch, t_new, heads, HEAD_DIM)
    qbd = jnp.einsum('bthe,hg->bhegt', q4, jnp.eye(heads, dtype=F32)).reshape(dec_batch, d, n_col)
    bias_cols = jnp.repeat(bias.astype(F32), t_new).reshape(1, n_col)
    pad = ((0, 0), (0, PAGE_SIZE - t_new), (0, 0))
    kn = jnp.pad(k_new.reshape(dec_batch, t_new, d), pad)
    vn = jnp.pad(v_new.reshape(dec_batch, t_new, d), pad)
    steps = n_pages // pages_per_step

    def pool_rows(c):
        return c.reshape(c.shape[0], c.shape[1], PAGE_SIZE * heads, HEAD_DIM)

    def page_spec(p):
        return pl.BlockSpec(
            (None, None, PAGE_SIZE * heads, HEAD_DIM),
            lambda b, s, pt, p=p: (layer, pt[b, n_pages - 1 - (s * pages_per_step + p)], 0, 0))

    grid_spec = pltpu.PrefetchScalarGridSpec(
        num_scalar_prefetch=1,
        grid=(dec_batch, steps),
        in_specs=[
            pl.BlockSpec((None, d, n_col), lambda b, s, pt: (b, 0, 0)),
            pl.BlockSpec((1, n_col), lambda b, s, pt: (0, 0)),
            pl.BlockSpec((None, PAGE_SIZE, d), lambda b, s, pt: (b, 0, 0)),
            pl.BlockSpec((None, PAGE_SIZE, d), lambda b, s, pt: (b, 0, 0)),
        ] + [page_spec(p) for p in range(pages_per_step)] * 2,
        out_specs=pl.BlockSpec((t_new, d), lambda b, s, pt: (b, 0)),
        scratch_shapes=[pltpu.VMEM((1, n_col), F32)],
    )
    return pl.pallas_call(
        functools.partial(_sb_sample_kernel, pages_per_step=pages_per_step, t_new=t_new, heads=heads),
        grid_spec=grid_spec,
        out_shape=jax.ShapeDtypeStruct((m, d), F32),
        compiler_params=_params("arbitrary", "arbitrary"),
        name="sb_sample_attention",
    )(page_table, qbd, bias_cols, kn, vn, *([pool_rows(cache_k)] * pages_per_step),
      *([pool_rows(cache_v)] * pages_per_step))


def _head_rows_kernel(*refs, heads):
    xs, o_ref = refs[:-1], refs[-1]
    layer = pl.program_id(0)
    tm = xs[0].shape[0]
    for l, x_ref in enumerate(xs):
        @pl.when(layer == l)
        def _(x_ref=x_ref):
            for h in range(heads):
                o_ref[pl.ds(h, tm, stride=heads), :] = x_ref[:, h * HEAD_DIM:(h + 1) * HEAD_DIM]


def stack_head_rows(xs, seg, batch, seq, win):
    _, _, d = xs[0].shape
    heads = d // HEAD_DIM
    tm = min(512, win)
    assert win <= seq and win % tm == 0 and (seq - win) % tm == 0
    per_seq, per_win, first = seq // tm, win // tm, (seq - win) // tm
    nt = batch * per_win

    def in_spec(l):
        def index(lay, t):
            t = jnp.where(lay == l, t, jnp.where(lay < l, 0, nt - 1))
            return seg, (t // per_win) * per_seq + first + t % per_win, 0
        return pl.BlockSpec((None, tm, d), index)

    out = pl.pallas_call(
        functools.partial(_head_rows_kernel, heads=heads),
        grid=(len(xs), nt),
        in_specs=[in_spec(l) for l in range(len(xs))],
        out_specs=pl.BlockSpec((None, tm * heads, HEAD_DIM), lambda lay, t: (lay, t, 0)),
        out_shape=jax.ShapeDtypeStruct((len(xs), batch * win * heads, HEAD_DIM), xs[0].dtype),
        compiler_params=_params("arbitrary", "arbitrary"),
        name="stack_head_rows",
    )(*xs)
    return out.reshape(len(xs), batch, win, heads, HEAD_DIM)


def _alibi_slopes(heads):
    n = len(SWA_DILATIONS) * heads
    s = 2.0 ** (-8.0 * np.arange(1, n + 1) / n)
    return jnp.asarray(s.reshape(len(SWA_DILATIONS), heads), dtype=F32)


def _lane_table(per_head, width):
    heads, n = per_head.shape
    tab = jnp.pad(per_head, ((0, 0), (0, SUBLANES - n)))
    return jnp.broadcast_to(tab[:, :, None], (heads, SUBLANES, width))


def _last_rows(t, n):
    seq = t.shape[1]
    if seq >= n:
        return t[:, seq - n:]
    return jnp.pad(t, ((0, 0), (n - seq, 0), (0, 0), (0, 0)))


def kernel(x_prompt, x_sample, cache_swa_k0, cache_swa_v0, cache_swa_k1, cache_swa_v1, cache_swa_k2, cache_swa_v2, cache_sb_k, cache_sb_v, page_table, c_prompt, c_sample, w_ada, b_ada, norm_mix_g, norm_ffn_g, w_qkv_swa, w_o_swa, w_qkv_sb, w_o_sb, sb_bias, w_gate, w_up, w_down, final_norm_g):
    batch, seq, d = x_prompt.shape
    dec_batch, t_new, _ = x_sample.shape
    depth = w_ada.shape[0]
    heads = d // HEAD_DIM
    n_pages = page_table.shape[1]
    past_len = n_pages * cache_sb_k.shape[2]
    mp_rows, ms_rows = batch * seq, dec_batch * t_new
    tm_p, tm_s = 1024, ms_rows
    sb_tq, sb_hps = 256, 8

    slopes_tab = _lane_table(_alibi_slopes(heads).T, LANES)
    swa_caches = (cache_swa_k0, cache_swa_v0, cache_swa_k1, cache_swa_v1, cache_swa_k2, cache_swa_v2)
    swa_s = [swa_shift_window(c, t_new) for c in swa_caches]

    n_c = batch + dec_batch
    c_rows = -(-n_c // SUBLANES) * SUBLANES
    c_all = jnp.pad(jnp.concatenate([c_prompt, c_sample], axis=0), ((0, c_rows - n_c), (0, 0)))
    mods = ada_all(c_all, w_ada, b_ada)

    xp = x_prompt.reshape(mp_rows, d)
    xs = x_sample.reshape(ms_rows, d)
    swa_qkv_p, sb_qkv_p = [], []
    sb_s = [[], []]

    for i in range(depth):
        mod_i = mods[i].reshape(c_rows, 6, d)
        mp = [mod_i[:batch, j][:, None, :] for j in range(6)]
        ms = [jnp.repeat(mod_i[batch:n_c, j], t_new, axis=0)[None] for j in range(6)]
        hp = norm_mod(xp, norm_mix_g[i], mp[0], mp[1], tm_p)
        hs = norm_mod(xs, norm_mix_g[i], ms[0], ms[1], tm_s)
        if i % 2 == 0:
            a = i // 2
            qkv_p = project_segments(hp, w_qkv_swa, a, d, tm_p, 1024)
            qkv_s = project_segments(hs, w_qkv_swa, a, d, tm_s, 1024)
            op = swa_prompt_attention(qkv_p, slopes_tab, batch, seq)
            os_ = swa_sample_attention(qkv_s, swa_caches, a, slopes_tab, dec_batch, t_new, past_len)
            news = [qkv_s[3 * g + 1 + c].reshape(dec_batch, t_new, heads, HEAD_DIM)
                    for g in range(len(SWA_WINDOWS)) for c in range(2)]
            swa_s = swa_write_new_rows(news, swa_s, a, t_new)
            swa_qkv_p.append(qkv_p)
            w_o, la = w_o_swa, a
        else:
            b = i // 2
            qkv_p = project_segments(hp, w_qkv_sb, b, d, tm_p, 1024, scaled_seg0=HEAD_DIM ** -0.5)
            qkv_s = project_segments(hs, w_qkv_sb, b, d, tm_s, 1024, scaled_seg0=HEAD_DIM ** -0.5)
            op = sb_prompt_attention(qkv_p, sb_bias[b], batch, seq, sb_tq, sb_hps)
            os_ = sb_sample_attention(qkv_s[0], qkv_s[1], qkv_s[2], cache_sb_k, cache_sb_v, b,
                                      page_table, sb_bias[b], t_new)
            sb_qkv_p.append(qkv_p)
            for c in range(2):
                sb_s[c].append(qkv_s[1 + c].reshape(dec_batch, t_new, heads, HEAD_DIM))
            w_o, la = w_o_sb, b
        xp = project_residual(op, w_o, la, xp, mp[2], tm_p, 1024)
        xs = project_residual(os_, w_o, la, xs, ms[2], tm_s, 1024)
        hp = norm_mod(xp, norm_ffn_g[i], mp[3], mp[4], tm_p)
        hs = norm_mod(xs, norm_ffn_g[i], ms[3], ms[4], tm_s)
        ap = project_swiglu(hp, w_gate, w_up, i, tm_p, 512)
        as_ = project_swiglu(hs, w_gate, w_up, i, tm_s, 512)
        xp = project_residual(ap, w_down, i, xp, mp[5], tm_p, 256)
        xs = project_residual(as_, w_down, i, xs, ms[5], tm_s, 256)

    y_prompt = final_norm(xp, final_norm_g, tm_p).reshape(batch, seq, d)
    y_sample = final_norm(xs, final_norm_g, tm_s).reshape(dec_batch, t_new, d)
    outs = [y_prompt, y_sample]
    for g, win in enumerate(SWA_WINDOWS):
        for c in range(2):
            seg = 3 * g + 1 + c
            if win <= seq:
                outs.append(stack_head_rows(swa_qkv_p, seg, batch, seq, win))
            else:
                outs.append(jnp.stack([_last_rows(q[seg].reshape(batch, seq, heads, HEAD_DIM), win)
                                       for q in swa_qkv_p]))
    outs += [stack_head_rows(sb_qkv_p, 1 + c, batch, seq, seq) for c in range(2)]
    outs += list(swa_s)
    outs += [jnp.stack(t) for t in sb_s]
    return tuple(outs)
```

```python
import functools
import math

import numpy as np
import jax
import jax.numpy as jnp
from jax import lax
from jax.experimental import pallas as pl
from jax.experimental.pallas import tpu as pltpu

F32 = jnp.float32
BF16 = jnp.bfloat16

HEAD_DIM = 128
SWA_WINDOWS = (128, 512, 2048)
SWA_DILATIONS = (1, 4, 16)
SWA_SPAN = 128
Q_BLOCK = 128
PAGE_SIZE = 128
EPS = 1e-6
MASK_VALUE = -1e30
LANES = 128
SUBLANES = 8
VMEM_LIMIT_BYTES = 56 * 1024 * 1024

NT_DIMS = (((1,), (1,)), ((), ()))
TN_DIMS = (((0,), (0,)), ((), ()))


def _params(*sem):
    return pltpu.CompilerParams(dimension_semantics=sem, vmem_limit_bytes=VMEM_LIMIT_BYTES)


def _dot(a, b, dims=None):
    if dims is None:
        return jnp.dot(a, b, preferred_element_type=F32)
    return lax.dot_general(a, b, dims, preferred_element_type=F32)


def _softplus(z):
    return jnp.maximum(z, 0.0) + jnp.log(1.0 + jnp.exp(-jnp.abs(z)))


def _split_bf16(x):
    hi = x.astype(BF16)
    lo = (x - hi.astype(F32)).astype(BF16)
    return hi, lo


def _ada_kernel(c_ref, w_ref, b_ref, o_ref):
    c = c_ref[...]
    a = (c / (1.0 + jnp.exp(-c))).astype(BF16)
    o_ref[...] = _dot(a, w_ref[...].astype(BF16)) + b_ref[...]


def ada_all(c_all, w_ada, b_ada, tn=1024):
    depth, d, n = w_ada.shape
    r = c_all.shape[0]
    return pl.pallas_call(
        _ada_kernel,
        grid=(depth, n // tn),
        in_specs=[
            pl.BlockSpec((r, d), lambda l, j: (0, 0)),
            pl.BlockSpec((None, d, tn), lambda l, j: (l, 0, j)),
            pl.BlockSpec((None, 1, tn), lambda l, j: (l, 0, j)),
        ],
        out_specs=pl.BlockSpec((None, r, tn), lambda l, j: (l, 0, j)),
        out_shape=jax.ShapeDtypeStruct((depth, r, n), F32),
        compiler_params=_params("arbitrary", "arbitrary"),
        name="ada_mod",
    )(c_all, w_ada, b_ada.reshape(depth, 1, n))


def _rms(x):
    return x * lax.rsqrt(jnp.mean(x * x, axis=-1, keepdims=True) + EPS)


def _norm_mod_kernel(x_ref, g_ref, sh_ref, sc_ref, o_ref):
    y = _rms(x_ref[...]) * g_ref[...]
    o_ref[...] = (y * (1.0 + sc_ref[...]) + sh_ref[...]).astype(o_ref.dtype)


def _norm_kernel(x_ref, g_ref, o_ref):
    o_ref[...] = (_rms(x_ref[...]) * g_ref[...]).astype(o_ref.dtype)


def _mod_spec(mod, tiles_per_group, width, two_d):
    r = mod.shape[1]
    if two_d:
        return pl.BlockSpec((None, r, width), lambda i, j: (i // tiles_per_group, 0, j))
    return pl.BlockSpec((None, r, width), lambda i: (i // tiles_per_group, 0, 0))


def norm_mod(x, g, shift, scale, tm):
    m, d = x.shape
    tpg = (m // tm) // shift.shape[0]
    return pl.pallas_call(
        _norm_mod_kernel,
        grid=(m // tm,),
        in_specs=[
            pl.BlockSpec((tm, d), lambda i: (i, 0)),
            pl.BlockSpec((1, d), lambda i: (0, 0)),
            _mod_spec(shift, tpg, d, False),
            _mod_spec(scale, tpg, d, False),
        ],
        out_specs=pl.BlockSpec((tm, d), lambda i: (i, 0)),
        out_shape=jax.ShapeDtypeStruct((m, d), BF16),
        compiler_params=_params("arbitrary"),
        name="norm_mod",
    )(x, g.reshape(1, d), shift, scale)


def final_norm(x, g, tm):
    m, d = x.shape
    return pl.pallas_call(
        _norm_kernel,
        grid=(m // tm,),
        in_specs=[pl.BlockSpec((tm, d), lambda i: (i, 0)), pl.BlockSpec((1, d), lambda i: (0, 0))],
        out_specs=pl.BlockSpec((tm, d), lambda i: (i, 0)),
        out_shape=jax.ShapeDtypeStruct((m, d), F32),
        compiler_params=_params("arbitrary"),
        name="final_norm",
    )(x, g.reshape(1, d))


def _proj_kernel(a_ref, w_ref, o_ref, *, scaled_tiles, scale):
    acc = _dot(a_ref[...].astype(BF16), w_ref[...].astype(BF16))
    if scaled_tiles:
        acc = acc * jnp.where(pl.program_id(1) < scaled_tiles, scale, 1.0).astype(F32)
    o_ref[...] = acc


def project_segments(a, w, layer, seg, tm, tn, scaled_seg0=None):
    m, k = a.shape
    n = w.shape[2]
    tps = seg // tn
    kern = functools.partial(
        _proj_kernel,
        scaled_tiles=tps if scaled_seg0 is not None else 0,
        scale=scaled_seg0 if scaled_seg0 is not None else 1.0,
    )
    return pl.pallas_call(
        kern,
        grid=(m // tm, n // tn),
        in_specs=[
            pl.BlockSpec((tm, k), lambda i, j: (i, 0)),
            pl.BlockSpec((None, k, tn), lambda i, j: (layer, 0, j)),
        ],
        out_specs=pl.BlockSpec((None, tm, tn), lambda i, j: (j // tps, i, j % tps)),
        out_shape=jax.ShapeDtypeStruct((n // seg, m, seg), F32),
        compiler_params=_params("arbitrary", "arbitrary"),
        name="project_segments",
    )(a, w)


def _residual_kernel(a_ref, w_ref, x_ref, g_ref, o_ref):
    acc = _dot(a_ref[...].astype(BF16), w_ref[...].astype(BF16))
    o_ref[...] = x_ref[...] + g_ref[...] * acc


def project_residual(a, w, layer, x, gate, tm, tn):
    m, k = a.shape
    n = w.shape[2]
    tpg = (m // tm) // gate.shape[0]
    return pl.pallas_call(
        _residual_kernel,
        grid=(m // tm, n // tn),
        in_specs=[
            pl.BlockSpec((tm, k), lambda i, j: (i, 0)),
            pl.BlockSpec((None, k, tn), lambda i, j: (layer, 0, j)),
            pl.BlockSpec((tm, tn), lambda i, j: (i, j)),
            _mod_spec(gate, tpg, tn, True),
        ],
        out_specs=pl.BlockSpec((tm, tn), lambda i, j: (i, j)),
        out_shape=jax.ShapeDtypeStruct((m, n), F32),
        compiler_params=_params("arbitrary", "arbitrary"),
        name="project_residual",
    )(a, w, x, gate)


def _swiglu_kernel(a_ref, wg_ref, wu_ref, o_ref):
    a = a_ref[...].astype(BF16)
    g = _dot(a, wg_ref[...].astype(BF16))
    u = _dot(a, wu_ref[...].astype(BF16))
    o_ref[...] = ((g / (1.0 + jnp.exp(-g))) * u).astype(o_ref.dtype)


def project_swiglu(a, w_gate, w_up, layer, tm, tn):
    m, k = a.shape
    n = w_gate.shape[2]
    wspec = pl.BlockSpec((None, k, tn), lambda i, j: (layer, 0, j))
    return pl.pallas_call(
        _swiglu_kernel,
        grid=(m // tm, n // tn),
        in_specs=[pl.BlockSpec((tm, k), lambda i, j: (i, 0)), wspec, wspec],
        out_specs=pl.BlockSpec((tm, tn), lambda i, j: (i, j)),
        out_shape=jax.ShapeDtypeStruct((m, n), BF16),
        compiler_params=_params("arbitrary", "arbitrary"),
        name="project_swiglu",
    )(a, w_gate, w_up)


def _rows(start, dil):
    return pl.ds(start, Q_BLOCK) if dil == 1 else pl.ds(start, Q_BLOCK, stride=dil)


def _pick_unroll(n):
    for u in (4, 5, 3, 2):
        if n % u == 0:
            return u
    return 1


def _swa_prompt_kernel(sl_ref, q0, k0, v0, q1, k1, v1, q2, k2, v2, o_ref,
                       og_scr, lse_scr, bias_scr, *, seq):
    qkv = ((q0, k0, v0), (q1, k1, v1), (q2, k2, v2))
    scale = HEAD_DIM ** -0.5
    qi = lax.broadcasted_iota(jnp.int32, (Q_BLOCK, Q_BLOCK), 0)
    kj = lax.broadcasted_iota(jnp.int32, (Q_BLOCK, Q_BLOCK), 1)

    for g, dil in enumerate(SWA_DILATIONS):
        slope = sl_ref[g:g + 1, :]
        d_cur = qi - kj
        d_prev = Q_BLOCK + qi - kj
        bias_scr[g, 0] = jnp.where(d_cur >= 0, -(slope * (d_cur * dil).astype(F32)), MASK_VALUE)
        bias_scr[g, 1] = jnp.where(d_prev <= SWA_SPAN, -(slope * (d_prev * dil).astype(F32)), MASK_VALUE)

    for g, dil in enumerate(SWA_DILATIONS):
        q_ref, k_ref, v_ref = qkv[g]
        blocks_per_res = seq // (dil * Q_BLOCK)

        def blocks(starts, with_prev, g=g, dil=dil, q_ref=q_ref, k_ref=k_ref, v_ref=v_ref):
            n = len(starts)
            rows = [_rows(s, dil) for s in starts]
            q = [q_ref[r, :].astype(BF16) for r in rows]
            kc = [k_ref[r, :].astype(BF16) for r in rows]
            vc = [v_ref[r, :].astype(BF16) for r in rows]
            if with_prev:
                prow = [_rows(s - Q_BLOCK * dil, dil) for s in starts]
                kp = [k_ref[r, :].astype(BF16) for r in prow]
                vp = [v_ref[r, :].astype(BF16) for r in prow]
            s_c = [_dot(q[i], kc[i], NT_DIMS) * scale + bias_scr[g, 0] for i in range(n)]
            m = [jnp.max(s, axis=-1, keepdims=True) for s in s_c]
            if with_prev:
                s_p = [_dot(q[i], kp[i], NT_DIMS) * scale + bias_scr[g, 1] for i in range(n)]
                m = [jnp.maximum(m[i], jnp.max(s_p[i], axis=-1, keepdims=True)) for i in range(n)]
            p_c = [jnp.exp(s_c[i] - m[i]) for i in range(n)]
            den = [jnp.sum(p, axis=-1, keepdims=True) for p in p_c]
            if with_prev:
                p_p = [jnp.exp(s_p[i] - m[i]) for i in range(n)]
                den = [den[i] + jnp.sum(p_p[i], axis=-1, keepdims=True) for i in range(n)]
            inv = [1.0 / d_ for d_ in den]
            o = [_dot((p_c[i] * inv[i]).astype(BF16), vc[i]) for i in range(n)]
            if with_prev:
                o = [o[i] + _dot((p_p[i] * inv[i]).astype(BF16), vp[i]) for i in range(n)]
            lse = [jnp.broadcast_to(m[i] + jnp.log(den[i]), (Q_BLOCK, LANES)) for i in range(n)]
            for i in range(n):
                og_scr[g, rows[i], :] = o[i]
                lse_scr[g, rows[i], :] = lse[i]

        def run(n_blocks, start_of, with_prev, blocks=blocks):
            unroll = _pick_unroll(n_blocks)

            def body(it, carry):
                blocks([start_of(it * unroll + u) for u in range(unroll)], with_prev)
                return carry

            lax.fori_loop(0, n_blocks // unroll, body, 0)

        run(dil, lambda r: r, False)
        if blocks_per_res > 1:
            per = blocks_per_res - 1
            run(dil * per, lambda it, per=per, dil=dil: it // per + (it % per + 1) * (Q_BLOCK * dil), True)

    chunk = 256

    def merge_body(c, carry):
        rows = pl.ds(pl.multiple_of(c * chunk, chunk), chunk)
        l0, l1, l2 = lse_scr[0, rows, :], lse_scr[1, rows, :], lse_scr[2, rows, :]
        mx = jnp.maximum(jnp.maximum(l0, l1), l2)
        w0, w1, w2 = jnp.exp(l0 - mx), jnp.exp(l1 - mx), jnp.exp(l2 - mx)
        inv = 1.0 / (w0 + w1 + w2)
        o = (w0 * inv) * og_scr[0, rows, :] + (w1 * inv) * og_scr[1, rows, :] + (w2 * inv) * og_scr[2, rows, :]
        o_ref[rows, :] = o.astype(o_ref.dtype)
        return carry

    lax.fori_loop(0, seq // chunk, merge_body, 0)


def swa_prompt_attention(qkv, slopes_tab, batch, seq):
    _, m, d = qkv.shape
    heads = d // HEAD_DIM
    specs = [pl.BlockSpec((None, SUBLANES, LANES), lambda b, h: (h, 0, 0))]
    for s in range(9):
        specs.append(pl.BlockSpec((None, seq, HEAD_DIM), lambda b, h, s=s: (s, b, h)))
    return pl.pallas_call(
        functools.partial(_swa_prompt_kernel, seq=seq),
        grid=(batch, heads),
        in_specs=specs,
        out_specs=pl.BlockSpec((seq, HEAD_DIM), lambda b, h: (b, h)),
        out_shape=jax.ShapeDtypeStruct((m, d), BF16),
        scratch_shapes=[
            pltpu.VMEM((3, seq, HEAD_DIM), F32),
            pltpu.VMEM((3, seq, LANES), F32),
            pltpu.VMEM((3, 2, Q_BLOCK, Q_BLOCK), F32),
        ],
        compiler_params=_params("arbitrary", "arbitrary"),
        name="swa_prompt_attention",
    )(slopes_tab, *([qkv] * 9))


SHIFT_BLOCK_ROWS = 8192


def _shift_rows_kernel(x_ref, o_ref, *, shift):
    last = pl.program_id(0) == pl.num_programs(0) - 1

    @pl.when(jnp.logical_not(last))
    def _():
        o_ref[...] = x_ref[...]

    @pl.when(last)
    def _():
        rows = o_ref.shape[0]
        o_ref[0:rows - shift, :] = x_ref[shift:rows, :]
        o_ref[rows - shift:rows, :] = x_ref[rows - shift:rows, :]


def swa_shift_window(cache, t_new):
    layers, dec_batch, win, heads, hd = cache.shape
    total = layers * dec_batch * win * heads
    shift = t_new * heads
    rows = min(SHIFT_BLOCK_ROWS, total)
    assert rows % shift == 0 and total % rows == 0
    per_block, last_start = rows // shift, (total - rows) // shift
    out = pl.pallas_call(
        functools.partial(_shift_rows_kernel, shift=shift),
        grid=(total // rows,),
        in_specs=[pl.BlockSpec((pl.Element(rows), pl.Element(hd)),
                               lambda i: (jnp.minimum(i * per_block + 1, last_start) * shift, 0))],
        out_specs=pl.BlockSpec((rows, hd), lambda i: (i, 0)),
        out_shape=jax.ShapeDtypeStruct((total, hd), cache.dtype),
        compiler_params=_params("arbitrary"),
        name="swa_shift_window",
    )(cache.reshape(total, hd))
    return out.reshape(cache.shape)


def _new_rows_kernel(*refs, n):
    news, outs = refs[:n], refs[2 * n:]
    for c in range(n):
        outs[c][...] = news[c][...]


def swa_write_new_rows(news, bufs, layer, t_new):
    n = len(bufs)
    dec_batch, _, heads, hd = news[0].shape
    specs = [pl.BlockSpec((None, t_new, heads, hd), lambda b: (b, 0, 0, 0))] * n
    specs += [pl.BlockSpec(memory_space=pl.ANY)] * n
    out_specs = [pl.BlockSpec((None, None, t_new, heads, hd),
                              lambda b, last=buf.shape[2] // t_new - 1: (layer, b, last, 0, 0)) for buf in bufs]
    return pl.pallas_call(
        functools.partial(_new_rows_kernel, n=n),
        grid=(dec_batch,),
        in_specs=specs,
        out_specs=out_specs,
        out_shape=[jax.ShapeDtypeStruct(b.shape, b.dtype) for b in bufs],
        input_output_aliases={n + c: c for c in range(n)},
        compiler_params=_params("arbitrary"),
        name="swa_write_new_rows",
    )(*news, *bufs)


def _swa_sample_kernel(sl_ref, r_ref, k0, v0, k1, v1, k2, v2, o_ref, m_scr, l_scr, acc_scr,
                       *, t_new, heads, past_len):
    step = pl.program_id(1)
    scale = HEAD_DIM ** -0.5
    chunks = k2.shape[0]
    near = ((k0, v0), (k1, v1))

    def update(state, q_row, k, v, slope, dist, valid):
        m, l, acc = state
        s = jnp.sum(k * q_row, axis=-1, keepdims=True) * scale - slope * dist.astype(F32)
        s = jnp.where(valid, s, MASK_VALUE)
        m_new = jnp.maximum(m, jnp.max(s, axis=0, keepdims=True))
        p = jnp.exp(s - m_new)
        alpha = jnp.exp(m - m_new)
        l = l * alpha + jnp.sum(p, axis=0, keepdims=True)
        acc = acc * alpha + jnp.sum(p * v, axis=0, keepdims=True)
        return m_new, l, acc

    def far_piece(state, h, t):
        win, dil = SWA_WINDOWS[2], SWA_DILATIONS[2]
        i = step * chunks + lax.broadcasted_iota(jnp.int32, (chunks, 1), 0)
        dist = win - dil * i
        valid = (dist > 0) & (dist <= SWA_SPAN * dil) & ((past_len - win) + t + dil * i >= 0)
        q_row = r_ref[2, 0, h, t:t + 1, :]
        slope = sl_ref[h, 2:3, 0:1]
        return update(state, q_row, k2[:, t * heads + h, :], v2[:, t * heads + h, :], slope, dist, valid)

    def first_body(h, carry):
        slabs = {}
        for g, (k_ref, v_ref) in enumerate(near):
            dil = SWA_DILATIONS[g]
            for first in range(min(dil, t_new)):
                rows = pl.ds(first * heads + h, SWA_SPAN, stride=dil * heads)
                slabs[g, first] = (k_ref[rows, :], v_ref[rows, :])
        for t in range(t_new):
            state = (jnp.full((1, 1), MASK_VALUE, F32), jnp.zeros((1, 1), F32), jnp.zeros((1, HEAD_DIM), F32))
            u = lax.broadcasted_iota(jnp.int32, (t_new, 1), 0)
            for g, dil in enumerate(SWA_DILATIONS):
                dist = t - u
                valid = (dist >= 0) & ((dist & (dil - 1)) == 0)
                state = update(state, r_ref[g, 0, h, t:t + 1, :], r_ref[g, 1, h], r_ref[g, 2, h],
                               sl_ref[h, g:g + 1, 0:1], dist, valid)
            i = lax.broadcasted_iota(jnp.int32, (SWA_SPAN, 1), 0)
            for g in range(len(near)):
                win, dil = SWA_WINDOWS[g], SWA_DILATIONS[g]
                first = t % dil
                tok = first + dil * i
                dist = win + t - tok
                valid = (dist > 0) & (dist <= SWA_SPAN * dil) & ((past_len - win) + tok >= 0)
                state = update(state, r_ref[g, 0, h, t:t + 1, :], *slabs[g, first],
                               sl_ref[h, g:g + 1, 0:1], dist, valid)
            m, l, acc = far_piece(state, h, t)
            m_scr[h, t:t + 1, :] = jnp.broadcast_to(m, (1, LANES))
            l_scr[h, t:t + 1, :] = jnp.broadcast_to(l, (1, LANES))
            acc_scr[h, t:t + 1, :] = acc
        return carry

    def last_body(h, carry):
        for t in range(t_new):
            state = (m_scr[h, t:t + 1, 0:1], l_scr[h, t:t + 1, 0:1], acc_scr[h, t:t + 1, :])
            m, l, acc = far_piece(state, h, t)
            o_ref[h, t:t + 1, :] = acc / l
        return carry

    @pl.when(step == 0)
    def _():
        lax.fori_loop(0, heads, first_body, 0)

    @pl.when(step == 1)
    def _():
        lax.fori_loop(0, heads, last_body, 0)


def swa_sample_attention(qkv_new, caches, layer, slopes_tab, dec_batch, t_new, past_len):
    _, m, d = qkv_new.shape
    heads = d // HEAD_DIM
    far_dil, far_win = SWA_DILATIONS[2], SWA_WINDOWS[2]
    assert 2 * t_new <= far_dil and SWA_DILATIONS[0] == 1 and t_new <= 2 * SWA_DILATIONS[1]
    r = qkv_new.reshape(3, 3, dec_batch, t_new, heads, HEAD_DIM).transpose(0, 1, 2, 4, 3, 5)
    near = [c.reshape(c.shape[0], dec_batch, c.shape[2] * heads, HEAD_DIM) for c in caches[:4]]
    n_chunks = far_win // far_dil
    far = [c.reshape(c.shape[0], dec_batch, n_chunks, far_dil * heads, HEAD_DIM) for c in caches[4:]]
    steps = 2
    specs = [
        pl.BlockSpec((heads, SUBLANES, LANES), lambda b, s: (0, 0, 0)),
        pl.BlockSpec((3, 3, None, heads, t_new, HEAD_DIM), lambda b, s: (0, 0, b, 0, 0, 0)),
    ]
    for c in near:
        specs.append(pl.BlockSpec((None, None, c.shape[2], HEAD_DIM), lambda b, s: (layer, b, 0, 0)))
    for c in far:
        specs.append(pl.BlockSpec((None, None, n_chunks // steps, far_dil * heads // 2, HEAD_DIM),
                                  lambda b, s: (layer, b, s, 0, 0)))
    o = pl.pallas_call(
        functools.partial(_swa_sample_kernel, t_new=t_new, heads=heads, past_len=past_len),
        grid=(dec_batch, steps),
        in_specs=specs,
        out_specs=pl.BlockSpec((None, heads, t_new, HEAD_DIM), lambda b, s: (b, 0, 0, 0)),
        out_shape=jax.ShapeDtypeStruct((dec_batch, heads, t_new, HEAD_DIM), F32),
        scratch_shapes=[pltpu.VMEM((heads, t_new, LANES), F32), pltpu.VMEM((heads, t_new, LANES), F32),
                        pltpu.VMEM((heads, t_new, HEAD_DIM), F32)],
        compiler_params=_params("arbitrary", "arbitrary"),
        name="swa_sample_attention",
    )(slopes_tab, r, *near, *far)
    return o.transpose(0, 2, 1, 3).reshape(m, d)


def _sb_prompt_kernel(b_ref, q_ref, k_ref, v_ref, o_ref, *, tq, hps):
    qb = pl.program_id(2)
    row = lax.broadcasted_iota(jnp.int32, (tq, tq), 0)
    col = lax.broadcasted_iota(jnp.int32, (tq, tq), 1)
    newer_mat = jnp.where(row > col, 1.0, 0.0).astype(BF16)
    causal = col < row

    def key_block(kb, carry, acc, masked):
        heads = range(hps)
        cols = [slice(h * HEAD_DIM, (h + 1) * HEAD_DIM) for h in heads]
        ks = pl.ds(pl.multiple_of(kb * tq, tq), tq)
        q = [q_ref[:, c].astype(BF16) for c in cols]
        k = [k_ref[ks, c].astype(BF16) for c in cols]
        v = [v_ref[ks, c].astype(BF16) for c in cols]
        z = [_dot(q[h], k[h], NT_DIMS) + b_ref[h:h + 1, :] for h in heads]
        log_keep = [-_softplus(z[h]) for h in heads]
        log_beta = [z[h] + log_keep[h] for h in heads]
        if masked:
            log_keep = [jnp.where(causal, lk, 0.0) for lk in log_keep]
        split = [_split_bf16(lk) for lk in log_keep]
        newer = [_dot(split[h][0], newer_mat) + _dot(split[h][1], newer_mat) + carry[h] for h in heads]
        a = [jnp.exp(log_beta[h] + newer[h]) for h in heads]
        if masked:
            a = [jnp.where(causal, x, 0.0) for x in a]
        acc = tuple(acc[h] + _dot(a[h].astype(BF16), v[h]) for h in heads)
        carry = tuple(carry[h] + jnp.sum(log_keep[h], axis=-1, keepdims=True) for h in heads)
        return carry, acc

    carry0 = tuple(jnp.zeros((tq, 1), F32) for _ in range(hps))
    acc0 = tuple(jnp.zeros((tq, HEAD_DIM), F32) for _ in range(hps))
    state = key_block(qb, carry0, acc0, True)
    state = lax.fori_loop(0, qb, lambda it, st: key_block(qb - 1 - it, st[0], st[1], False), state)
    for h in range(hps):
        o_ref[:, h * HEAD_DIM:(h + 1) * HEAD_DIM] = state[1][h].astype(o_ref.dtype)


def sb_prompt_attention(qkv, bias, batch, seq, tq, hps):
    _, m, d = qkv.shape
    heads = d // HEAD_DIM
    nq = seq // tq
    width = hps * HEAD_DIM
    bias_tab = _lane_table(bias.astype(F32).reshape(heads // hps, hps), tq)
    return pl.pallas_call(
        functools.partial(_sb_prompt_kernel, tq=tq, hps=hps),
        grid=(batch, heads // hps, nq),
        in_specs=[
            pl.BlockSpec((None, SUBLANES, tq), lambda b, h, i: (h, 0, 0)),
            pl.BlockSpec((None, tq, width), lambda b, h, i: (0, b * nq + i, h)),
            pl.BlockSpec((None, seq, width), lambda b, h, i: (1, b, h)),
            pl.BlockSpec((None, seq, width), lambda b, h, i: (2, b, h)),
        ],
        out_specs=pl.BlockSpec((tq, width), lambda b, h, i: (b * nq + i, h)),
        out_shape=jax.ShapeDtypeStruct((m, d), BF16),
        compiler_params=_params("arbitrary", "arbitrary", "arbitrary"),
        name="sb_prompt_attention",
    )(bias_tab, qkv, qkv, qkv)


def _sb_sample_kernel(pt_ref, qbd_ref, bias_ref, kn_ref, vn_ref, *refs, pages_per_step, t_new, heads):
    del pt_ref
    k_refs = refs[:pages_per_step]
    v_refs = refs[pages_per_step:2 * pages_per_step]
    o_ref = refs[2 * pages_per_step]
    carry_scr = refs[2 * pages_per_step + 1]
    step = pl.program_id(1)
    qbd = qbd_ref[...].astype(BF16)
    bias = bias_ref[...]
    n_col = heads * t_new
    key = lax.broadcasted_iota(jnp.int32, (PAGE_SIZE, n_col), 0)
    tok = lax.broadcasted_iota(jnp.int32, (PAGE_SIZE, n_col), 1) % t_new
    row = lax.broadcasted_iota(jnp.int32, (PAGE_SIZE, PAGE_SIZE), 0)
    col = lax.broadcasted_iota(jnp.int32, (PAGE_SIZE, PAGE_SIZE), 1)
    newer_mat = jnp.where(col > row, 1.0, 0.0).astype(BF16)

    def pages(loaders, masked):
        n = len(loaders)
        k = [jnp.concatenate([hk(h) for h in range(heads)], axis=1).astype(BF16) for hk, _ in loaders]
        z = [_dot(k[i], qbd) + bias for i in range(n)]
        log_keep = [-_softplus(x) for x in z]
        log_beta = [z[i] + log_keep[i] for i in range(n)]
        if masked:
            causal = key < tok
            log_keep = [jnp.where(causal, lk, 0.0) for lk in log_keep]
        split = [_split_bf16(lk) for lk in log_keep]
        within = [_dot(newer_mat, hi) + _dot(newer_mat, lo) for hi, lo in split]
        carry = carry_scr[...]
        a_t = []
        for i in range(n):
            a = jnp.exp(log_beta[i] + (within[i] + carry))
            if masked:
                a = jnp.where(causal, a, 0.0)
            a_t.append(a.T)
            carry = carry + jnp.sum(log_keep[i], axis=0, keepdims=True)
        carry_scr[...] = carry
        for h in range(heads):
            rows = slice(h * t_new, (h + 1) * t_new)
            o = _dot(a_t[0][rows, :].astype(BF16), loaders[0][1](h).astype(BF16))
            for i in range(1, n):
                o = o + _dot(a_t[i][rows, :].astype(BF16), loaders[i][1](h).astype(BF16))
            o_ref[:, h * HEAD_DIM:(h + 1) * HEAD_DIM] += o

    def head_rows(ref):
        return lambda h: ref[pl.ds(h, PAGE_SIZE, stride=heads), :]

    @pl.when(step == 0)
    def _():
        o_ref[...] = jnp.zeros_like(o_ref)
        carry_scr[...] = jnp.zeros_like(carry_scr)
        pages([(lambda h: kn_ref[:, h * HEAD_DIM:(h + 1) * HEAD_DIM],
                lambda h: vn_ref[:, h * HEAD_DIM:(h + 1) * HEAD_DIM])], True)

    pages([(head_rows(k_refs[p]), head_rows(v_refs[p])) for p in range(pages_per_step)], False)


def sb_sample_attention(q_new, k_new, v_new, cache_k, cache_v, layer, page_table, bias, t_new, pages_per_step=8):
    m, d = q_new.shape
    heads = d // HEAD_DIM
    dec_batch, n_pages = page_table.shape
    n_col = heads * t_new
    q4 = q_new.reshape(dec_batch, t_new, heads, HEAD_DIM)
    qbd = jnp.einsum('bthe,hg->bhegt', q4, jnp.eye(heads, dtype=F32)).reshape(dec_batch, d, n_col)
    bias_cols = jnp.repeat(bias.astype(F32), t_new).reshape(1, n_col)
    pad = ((0, 0), (0, PAGE_SIZE - t_new), (0, 0))
    kn = jnp.pad(k_new.reshape(dec_batch, t_new, d), pad)
    vn = jnp.pad(v_new.reshape(dec_batch, t_new, d), pad)
    steps = n_pages // pages_per_step

    def pool_rows(c):
        return c.reshape(c.shape[0], c.shape[1], PAGE_SIZE * heads, HEAD_DIM)

    def page_spec(p):
        return pl.BlockSpec(
            (None, None, PAGE_SIZE * heads, HEAD_DIM),
            lambda b, s, pt, p=p: (layer, pt[b, n_pages - 1 - (s * pages_per_step + p)], 0, 0))

    grid_spec = pltpu.PrefetchScalarGridSpec(
        num_scalar_prefetch=1,
        grid=(dec_batch, steps),
        in_specs=[
            pl.BlockSpec((None, d, n_col), lambda b, s, pt: (b, 0, 0)),
            pl.BlockSpec((1, n_col), lambda b, s, pt: (0, 0)),
            pl.BlockSpec((None, PAGE_SIZE, d), lambda b, s, pt: (b, 0, 0)),
            pl.BlockSpec((None, PAGE_SIZE, d), lambda b, s, pt: (b, 0, 0)),
        ] + [page_spec(p) for p in range(pages_per_step)] * 2,
        out_specs=pl.BlockSpec((t_new, d), lambda b, s, pt: (b, 0)),
        scratch_shapes=[pltpu.VMEM((1, n_col), F32)],
    )
    return pl.pallas_call(
        functools.partial(_sb_sample_kernel, pages_per_step=pages_per_step, t_new=t_new, heads=heads),
        grid_spec=grid_spec,
        out_shape=jax.ShapeDtypeStruct((m, d), F32),
        compiler_params=_params("arbitrary", "arbitrary"),
        name="sb_sample_attention",
    )(page_table, qbd, bias_cols, kn, vn, *([pool_rows(cache_k)] * pages_per_step),
      *([pool_rows(cache_v)] * pages_per_step))


def _head_rows_kernel(*refs, heads):
    xs, o_ref = refs[:-1], refs[-1]
    layer = pl.program_id(0)
    tm = xs[0].shape[0]
    for l, x_ref in enumerate(xs):
        @pl.when(layer == l)
        def _(x_ref=x_ref):
            for h in range(heads):
                o_ref[pl.ds(h, tm, stride=heads), :] = x_ref[:, h * HEAD_DIM:(h + 1) * HEAD_DIM]


def stack_head_rows(xs, seg, batch, seq, win):
    _, _, d = xs[0].shape
    heads = d // HEAD_DIM
    tm = min(512, win)
    assert win <= seq and win % tm == 0 and (seq - win) % tm == 0
    per_seq, per_win, first = seq // tm, win // tm, (seq - win) // tm
    nt = batch * per_win

    def in_spec(l):
        def index(lay, t):
            t = jnp.where(lay == l, t, jnp.where(lay < l, 0, nt - 1))
            return seg, (t // per_win) * per_seq + first + t % per_win, 0
        return pl.BlockSpec((None, tm, d), index)

    out = pl.pallas_call(
        functools.partial(_head_rows_kernel, heads=heads),
        grid=(len(xs), nt),
        in_specs=[in_spec(l) for l in range(len(xs))],
        out_specs=pl.BlockSpec((None, tm * heads, HEAD_DIM), lambda lay, t: (lay, t, 0)),
        out_shape=jax.ShapeDtypeStruct((len(xs), batch * win * heads, HEAD_DIM), xs[0].dtype),
        compiler_params=_params("arbitrary", "arbitrary"),
        name="stack_head_rows",
    )(*xs)
    return out.reshape(len(xs), batch, win, heads, HEAD_DIM)


def _alibi_slopes(heads):
    n = len(SWA_DILATIONS) * heads
    s = 2.0 ** (-8.0 * np.arange(1, n + 1) / n)
    return jnp.asarray(s.reshape(len(SWA_DILATIONS), heads), dtype=F32)


def _lane_table(per_head, width):
    heads, n = per_head.shape
    tab = jnp.pad(per_head, ((0, 0), (0, SUBLANES - n)))
    return jnp.broadcast_to(tab[:, :, None], (heads, SUBLANES, width))


def _last_rows(t, n):
    seq = t.shape[1]
    if seq >= n:
        return t[:, seq - n:]
    return jnp.pad(t, ((0, 0), (n - seq, 0), (0, 0), (0, 0)))


def kernel(x_prompt, x_sample, cache_swa_k0, cache_swa_v0, cache_swa_k1, cache_swa_v1, cache_swa_k2, cache_swa_v2, cache_sb_k, cache_sb_v, page_table, c_prompt, c_sample, w_ada, b_ada, norm_mix_g, norm_ffn_g, w_qkv_swa, w_o_swa, w_qkv_sb, w_o_sb, sb_bias, w_gate, w_up, w_down, final_norm_g):
    batch, seq, d = x_prompt.shape
    dec_batch, t_new, _ = x_sample.shape
    depth = w_ada.shape[0]
    heads = d // HEAD_DIM
    n_pages = page_table.shape[1]
    past_len = n_pages * cache_sb_k.shape[2]
    mp_rows, ms_rows = batch * seq, dec_batch * t_new
    tm_p, tm_s = 1024, ms_rows
    sb_tq, sb_hps = 256, 8

    slopes_tab = _lane_table(_alibi_slopes(heads).T, LANES)
    swa_caches = (cache_swa_k0, cache_swa_v0, cache_swa_k1, cache_swa_v1, cache_swa_k2, cache_swa_v2)
    swa_s = [swa_shift_window(c, t_new) for c in swa_caches]

    n_c = batch + dec_batch
    c_rows = -(-n_c // SUBLANES) * SUBLANES
    c_all = jnp.pad(jnp.concatenate([c_prompt, c_sample], axis=0), ((0, c_rows - n_c), (0, 0)))
    mods = ada_all(c_all, w_ada, b_ada)

    xp = x_prompt.reshape(mp_rows, d)
    xs = x_sample.reshape(ms_rows, d)
    swa_qkv_p, sb_qkv_p = [], []
    sb_s = [[], []]

    for i in range(depth):
        mod_i = mods[i].reshape(c_rows, 6, d)
        mp = [mod_i[:batch, j][:, None, :] for j in range(6)]
        ms = [jnp.repeat(mod_i[batch:n_c, j], t_new, axis=0)[None] for j in range(6)]
        hp = norm_mod(xp, norm_mix_g[i], mp[0], mp[1], tm_p)
        hs = norm_mod(xs, norm_mix_g[i], ms[0], ms[1], tm_s)
        if i % 2 == 0:
            a = i // 2
            qkv_p = project_segments(hp, w_qkv_swa, a, d, tm_p, 1024)
            qkv_s = project_segments(hs, w_qkv_swa, a, d, tm_s, 1024)
            op = swa_prompt_attention(qkv_p, slopes_tab, batch, seq)
            os_ = swa_sample_attention(qkv_s, swa_caches, a, slopes_tab, dec_batch, t_new, past_len)
            news = [qkv_s[3 * g + 1 + c].reshape(dec_batch, t_new, heads, HEAD_DIM)
                    for g in range(len(SWA_WINDOWS)) for c in range(2)]
            swa_s = swa_write_new_rows(news, swa_s, a, t_new)
            swa_qkv_p.append(qkv_p)
            w_o, la = w_o_swa, a
        else:
            b = i // 2
            qkv_p = project_segments(hp, w_qkv_sb, b, d, tm_p, 1024, scaled_seg0=HEAD_DIM ** -0.5)
            qkv_s = project_segments(hs, w_qkv_sb, b, d, tm_s, 1024, scaled_seg0=HEAD_DIM ** -0.5)
            op = sb_prompt_attention(qkv_p, sb_bias[b], batch, seq, sb_tq, sb_hps)
            os_ = sb_sample_attention(qkv_s[0], qkv_s[1], qkv_s[2], cache_sb_k, cache_sb_v, b,
                                      page_table, sb_bias[b], t_new)
            sb_qkv_p.append(qkv_p)
            for c in range(2):
                sb_s[c].append(qkv_s[1 + c].reshape(dec_batch, t_new, heads, HEAD_DIM))
            w_o, la = w_o_sb, b
        xp = project_residual(op, w_o, la, xp, mp[2], tm_p, 1024)
        xs = project_residual(os_, w_o, la, xs, ms[2], tm_s, 1024)
        hp = norm_mod(xp, norm_ffn_g[i], mp[3], mp[4], tm_p)
        hs = norm_mod(xs, norm_ffn_g[i], ms[3], ms[4], tm_s)
        ap = project_swiglu(hp, w_gate, w_up, i, tm_p, 512)
        as_ = project_swiglu(hs, w_gate, w_up, i, tm_s, 512)
        xp = project_residual(ap, w_down, i, xp, mp[5], tm_p, 256)
        xs = project_residual(as_, w_down, i, xs, ms[5], tm_s, 256)

    y_prompt = final_norm(xp, final_norm_g, tm_p).reshape(batch, seq, d)
    y_sample = final_norm(xs, final_norm_g, tm_s).reshape(dec_batch, t_new, d)
    outs = [y_prompt, y_sample]
    for g, win in enumerate(SWA_WINDOWS):
        for c in range(2):
            seg = 3 * g + 1 + c
            if win <= seq:
                outs.append(stack_head_rows(swa_qkv_p, seg, batch, seq, win))
            else:
                outs.append(jnp.stack([_last_rows(q[seg].reshape(batch, seq, heads, HEAD_DIM), win)
                                       for q in swa_qkv_p]))
    outs += [stack_head_rows(sb_qkv_p, 1 + c, batch, seq, seq) for c in range(2)]
    outs += list(swa_s)
    outs += [jnp.stack(t) for t in sb_s]
    return tuple(outs)
```

```python
import functools
import math

import numpy as np
import jax
import jax.numpy as jnp
from jax import lax
from jax.experimental import pallas as pl
from jax.experimental.pallas import tpu as pltpu

F32 = jnp.float32
BF16 = jnp.bfloat16

HEAD_DIM = 128
SWA_WINDOWS = (128, 512, 2048)
SWA_DILATIONS = (1, 4, 16)
SWA_SPAN = 128
Q_BLOCK = 128
PAGE_SIZE = 128
EPS = 1e-6
MASK_VALUE = -1e30
LANES = 128
SUBLANES = 8
VMEM_LIMIT_BYTES = 56 * 1024 * 1024

NT_DIMS = (((1,), (1,)), ((), ()))
TN_DIMS = (((0,), (0,)), ((), ()))


def _params(*sem):
    return pltpu.CompilerParams(dimension_semantics=sem, vmem_limit_bytes=VMEM_LIMIT_BYTES)


def _dot(a, b, dims=None):
    if dims is None:
        return jnp.dot(a, b, preferred_element_type=F32)
    return lax.dot_general(a, b, dims, preferred_element_type=F32)


def _softplus(z):
    return jnp.maximum(z, 0.0) + jnp.log(1.0 + jnp.exp(-jnp.abs(z)))


def _split_bf16(x):
    hi = x.astype(BF16)
    lo = (x - hi.astype(F32)).astype(BF16)
    return hi, lo


def _ada_kernel(c_ref, w_ref, b_ref, o_ref):
    c = c_ref[...]
    a = (c / (1.0 + jnp.exp(-c))).astype(BF16)
    o_ref[...] = _dot(a, w_ref[...].astype(BF16)) + b_ref[...]


def ada_all(c_all, w_ada, b_ada, tn=1024):
    depth, d, n = w_ada.shape
    r = c_all.shape[0]
    return pl.pallas_call(
        _ada_kernel,
        grid=(depth, n // tn),
        in_specs=[
            pl.BlockSpec((r, d), lambda l, j: (0, 0)),
            pl.BlockSpec((None, d, tn), lambda l, j: (l, 0, j)),
            pl.BlockSpec((None, 1, tn), lambda l, j: (l, 0, j)),
        ],
        out_specs=pl.BlockSpec((None, r, tn), lambda l, j: (l, 0, j)),
        out_shape=jax.ShapeDtypeStruct((depth, r, n), F32),
        compiler_params=_params("arbitrary", "arbitrary"),
        name="ada_mod",
    )(c_all, w_ada, b_ada.reshape(depth, 1, n))


def _rms(x):
    return x * lax.rsqrt(jnp.mean(x * x, axis=-1, keepdims=True) + EPS)


def _norm_mod_kernel(x_ref, g_ref, sh_ref, sc_ref, o_ref):
    y = _rms(x_ref[...]) * g_ref[...]
    o_ref[...] = (y * (1.0 + sc_ref[...]) + sh_ref[...]).astype(o_ref.dtype)


def _norm_kernel(x_ref, g_ref, o_ref):
    o_ref[...] = (_rms(x_ref[...]) * g_ref[...]).astype(o_ref.dtype)


def _mod_spec(mod, tiles_per_group, width, two_d):
    r = mod.shape[1]
    if two_d:
        return pl.BlockSpec((None, r, width), lambda i, j: (i // tiles_per_group, 0, j))
    return pl.BlockSpec((None, r, width), lambda i: (i // tiles_per_group, 0, 0))


def norm_mod(x, g, shift, scale, tm):
    m, d = x.shape
    tpg = (m // tm) // shift.shape[0]
    return pl.pallas_call(
        _norm_mod_kernel,
        grid=(m // tm,),
        in_specs=[
            pl.BlockSpec((tm, d), lambda i: (i, 0)),
            pl.BlockSpec((1, d), lambda i: (0, 0)),
            _mod_spec(shift, tpg, d, False),
            _mod_spec(scale, tpg, d, False),
        ],
        out_specs=pl.BlockSpec((tm, d), lambda i: (i, 0)),
        out_shape=jax.ShapeDtypeStruct((m, d), BF16),
        compiler_params=_params("arbitrary"),
        name="norm_mod",
    )(x, g.reshape(1, d), shift, scale)


def final_norm(x, g, tm):
    m, d = x.shape
    return pl.pallas_call(
        _norm_kernel,
        grid=(m // tm,),
        in_specs=[pl.BlockSpec((tm, d), lambda i: (i, 0)), pl.BlockSpec((1, d), lambda i: (0, 0))],
        out_specs=pl.BlockSpec((tm, d), lambda i: (i, 0)),
        out_shape=jax.ShapeDtypeStruct((m, d), F32),
        compiler_params=_params("arbitrary"),
        name="final_norm",
    )(x, g.reshape(1, d))


def _proj_kernel(a_ref, w_ref, o_ref, *, scaled_tiles, scale):
    acc = _dot(a_ref[...].astype(BF16), w_ref[...].astype(BF16))
    if scaled_tiles:
        acc = acc * jnp.where(pl.program_id(1) < scaled_tiles, scale, 1.0).astype(F32)
    o_ref[...] = acc


def project_segments(a, w, layer, seg, tm, tn, scaled_seg0=None):
    m, k = a.shape
    n = w.shape[2]
    tps = seg // tn
    kern = functools.partial(
        _proj_kernel,
        scaled_tiles=tps if scaled_seg0 is not None else 0,
        scale=scaled_seg0 if scaled_seg0 is not None else 1.0,
    )
    return pl.pallas_call(
        kern,
        grid=(m // tm, n // tn),
        in_specs=[
            pl.BlockSpec((tm, k), lambda i, j: (i, 0)),
            pl.BlockSpec((None, k, tn), lambda i, j: (layer, 0, j)),
        ],
        out_specs=pl.BlockSpec((None, tm, tn), lambda i, j: (j // tps, i, j % tps)),
        out_shape=jax.ShapeDtypeStruct((n // seg, m, seg), F32),
        compiler_params=_params("arbitrary", "arbitrary"),
        name="project_segments",
    )(a, w)


def _normed_rows(x_ref, g_ref, sh_ref, sc_ref, h_scr):
    @pl.when(pl.program_id(1) == 0)
    def _():
        y = _rms(x_ref[...]) * g_ref[...]
        h_scr[...] = (y * (1.0 + sc_ref[...]) + sh_ref[...]).astype(h_scr.dtype)
    return h_scr[...]


def _norm_proj_kernel(x_ref, g_ref, sh_ref, sc_ref, w_ref, o_ref, h_scr, *, scaled_tiles, scale):
    acc = _dot(_normed_rows(x_ref, g_ref, sh_ref, sc_ref, h_scr), w_ref[...].astype(BF16))
    if scaled_tiles:
        acc = acc * jnp.where(pl.program_id(1) < scaled_tiles, scale, 1.0).astype(F32)
    o_ref[...] = acc


def _norm_swiglu_kernel(x_ref, g_ref, sh_ref, sc_ref, wg_ref, wu_ref, o_ref, h_scr):
    a = _normed_rows(x_ref, g_ref, sh_ref, sc_ref, h_scr)
    g = _dot(a, wg_ref[...].astype(BF16))
    u = _dot(a, wu_ref[...].astype(BF16))
    o_ref[...] = ((g / (1.0 + jnp.exp(-g))) * u).astype(o_ref.dtype)


def _norm_specs(x, g, shift, tm):
    m, d = x.shape
    tpg = (m // tm) // shift.shape[0]
    mod = pl.BlockSpec((None, shift.shape[1], d), lambda i, j: (i // tpg, 0, 0))
    return [pl.BlockSpec((tm, d), lambda i, j: (i, 0)), pl.BlockSpec((1, d), lambda i, j: (0, 0)), mod, mod]


def norm_project_segments(x, g, shift, scale, w, layer, seg, tm, tn, scaled_seg0=None):
    m, k = x.shape
    n = w.shape[2]
    tps = seg // tn
    kern = functools.partial(
        _norm_proj_kernel,
        scaled_tiles=tps if scaled_seg0 is not None else 0,
        scale=scaled_seg0 if scaled_seg0 is not None else 1.0,
    )
    return pl.pallas_call(
        kern,
        grid=(m // tm, n // tn),
        in_specs=_norm_specs(x, g, shift, tm) + [pl.BlockSpec((None, k, tn), lambda i, j: (layer, 0, j))],
        out_specs=pl.BlockSpec((None, tm, tn), lambda i, j: (j // tps, i, j % tps)),
        out_shape=jax.ShapeDtypeStruct((n // seg, m, seg), F32),
        scratch_shapes=[pltpu.VMEM((tm, k), BF16)],
        compiler_params=_params("arbitrary", "arbitrary"),
        name="norm_project_segments",
    )(x, g.reshape(1, k), shift, scale, w)


def norm_project_swiglu(x, g, shift, scale, w_gate, w_up, layer, tm, tn):
    m, k = x.shape
    n = w_gate.shape[2]
    wspec = pl.BlockSpec((None, k, tn), lambda i, j: (layer, 0, j))
    return pl.pallas_call(
        _norm_swiglu_kernel,
        grid=(m // tm, n // tn),
        in_specs=_norm_specs(x, g, shift, tm) + [wspec, wspec],
        out_specs=pl.BlockSpec((tm, tn), lambda i, j: (i, j)),
        out_shape=jax.ShapeDtypeStruct((m, n), BF16),
        scratch_shapes=[pltpu.VMEM((tm, k), BF16)],
        compiler_params=_params("arbitrary", "arbitrary"),
        name="norm_project_swiglu",
    )(x, g.reshape(1, k), shift, scale, w_gate, w_up)


def _residual_kernel(a_ref, w_ref, x_ref, g_ref, o_ref):
    acc = _dot(a_ref[...].astype(BF16), w_ref[...].astype(BF16))
    o_ref[...] = x_ref[...] + g_ref[...] * acc


def project_residual(a, w, layer, x, gate, tm, tn):
    m, k = a.shape
    n = w.shape[2]
    tpg = (m // tm) // gate.shape[0]
    return pl.pallas_call(
        _residual_kernel,
        grid=(m // tm, n // tn),
        in_specs=[
            pl.BlockSpec((tm, k), lambda i, j: (i, 0)),
            pl.BlockSpec((None, k, tn), lambda i, j: (layer, 0, j)),
            pl.BlockSpec((tm, tn), lambda i, j: (i, j)),
            _mod_spec(gate, tpg, tn, True),
        ],
        out_specs=pl.BlockSpec((tm, tn), lambda i, j: (i, j)),
        out_shape=jax.ShapeDtypeStruct((m, n), F32),
        compiler_params=_params("arbitrary", "arbitrary"),
        name="project_residual",
    )(a, w, x, gate)


def _swiglu_kernel(a_ref, wg_ref, wu_ref, o_ref):
    a = a_ref[...].astype(BF16)
    g = _dot(a, wg_ref[...].astype(BF16))
    u = _dot(a, wu_ref[...].astype(BF16))
    o_ref[...] = ((g / (1.0 + jnp.exp(-g))) * u).astype(o_ref.dtype)


def project_swiglu(a, w_gate, w_up, layer, tm, tn):
    m, k = a.shape
    n = w_gate.shape[2]
    wspec = pl.BlockSpec((None, k, tn), lambda i, j: (layer, 0, j))
    return pl.pallas_call(
        _swiglu_kernel,
        grid=(m // tm, n // tn),
        in_specs=[pl.BlockSpec((tm, k), lambda i, j: (i, 0)), wspec, wspec],
        out_specs=pl.BlockSpec((tm, tn), lambda i, j: (i, j)),
        out_shape=jax.ShapeDtypeStruct((m, n), BF16),
        compiler_params=_params("arbitrary", "arbitrary"),
        name="project_swiglu",
    )(a, w_gate, w_up)


def _rows(start, dil):
    return pl.ds(start, Q_BLOCK) if dil == 1 else pl.ds(start, Q_BLOCK, stride=dil)


def _pick_unroll(n):
    for u in (4, 5, 3, 2):
        if n % u == 0:
            return u
    return 1


def _swa_prompt_kernel(sl_ref, q0, k0, v0, q1, k1, v1, q2, k2, v2, o_ref,
                       og_scr, lse_scr, bias_scr, *, seq):
    qkv = ((q0, k0, v0), (q1, k1, v1), (q2, k2, v2))
    scale = HEAD_DIM ** -0.5
    qi = lax.broadcasted_iota(jnp.int32, (Q_BLOCK, Q_BLOCK), 0)
    kj = lax.broadcasted_iota(jnp.int32, (Q_BLOCK, Q_BLOCK), 1)

    for g, dil in enumerate(SWA_DILATIONS):
        slope = sl_ref[g:g + 1, :]
        d_cur = qi - kj
        d_prev = Q_BLOCK + qi - kj
        bias_scr[g, 0] = jnp.where(d_cur >= 0, -(slope * (d_cur * dil).astype(F32)), MASK_VALUE)
        bias_scr[g, 1] = jnp.where(d_prev <= SWA_SPAN, -(slope * (d_prev * dil).astype(F32)), MASK_VALUE)

    for g, dil in enumerate(SWA_DILATIONS):
        q_ref, k_ref, v_ref = qkv[g]
        blocks_per_res = seq // (dil * Q_BLOCK)

        def blocks(starts, with_prev, g=g, dil=dil, q_ref=q_ref, k_ref=k_ref, v_ref=v_ref):
            n = len(starts)
            rows = [_rows(s, dil) for s in starts]
            q = [q_ref[r, :].astype(BF16) for r in rows]
            kc = [k_ref[r, :].astype(BF16) for r in rows]
            vc = [v_ref[r, :].astype(BF16) for r in rows]
            if with_prev:
                prow = [_rows(s - Q_BLOCK * dil, dil) for s in starts]
                kp = [k_ref[r, :].astype(BF16) for r in prow]
                vp = [v_ref[r, :].astype(BF16) for r in prow]
            s_c = [_dot(q[i], kc[i], NT_DIMS) * scale + bias_scr[g, 0] for i in range(n)]
            m = [jnp.max(s, axis=-1, keepdims=True) for s in s_c]
            if with_prev:
                s_p = [_dot(q[i], kp[i], NT_DIMS) * scale + bias_scr[g, 1] for i in range(n)]
                m = [jnp.maximum(m[i], jnp.max(s_p[i], axis=-1, keepdims=True)) for i in range(n)]
            p_c = [jnp.exp(s_c[i] - m[i]) for i in range(n)]
            den = [jnp.sum(p, axis=-1, keepdims=True) for p in p_c]
            if with_prev:
                p_p = [jnp.exp(s_p[i] - m[i]) for i in range(n)]
                den = [den[i] + jnp.sum(p_p[i], axis=-1, keepdims=True) for i in range(n)]
            inv = [1.0 / d_ for d_ in den]
            o = [_dot((p_c[i] * inv[i]).astype(BF16), vc[i]) for i in range(n)]
            if with_prev:
                o = [o[i] + _dot((p_p[i] * inv[i]).astype(BF16), vp[i]) for i in range(n)]
            lse = [jnp.broadcast_to(m[i] + jnp.log(den[i]), (Q_BLOCK, LANES)) for i in range(n)]
            for i in range(n):
                og_scr[g, rows[i], :] = o[i]
                lse_scr[g, rows[i], :] = lse[i]

        def run(n_blocks, start_of, with_prev, blocks=blocks):
            unroll = _pick_unroll(n_blocks)

            def body(it, carry):
                blocks([start_of(it * unroll + u) for u in range(unroll)], with_prev)
                return carry

            lax.fori_loop(0, n_blocks // unroll, body, 0)

        run(dil, lambda r: r, False)
        if blocks_per_res > 1:
            per = blocks_per_res - 1
            run(dil * per, lambda it, per=per, dil=dil: it // per + (it % per + 1) * (Q_BLOCK * dil), True)

    chunk = 256

    def merge_body(c, carry):
        rows = pl.ds(pl.multiple_of(c * chunk, chunk), chunk)
        l0, l1, l2 = lse_scr[0, rows, :], lse_scr[1, rows, :], lse_scr[2, rows, :]
        mx = jnp.maximum(jnp.maximum(l0, l1), l2)
        w0, w1, w2 = jnp.exp(l0 - mx), jnp.exp(l1 - mx), jnp.exp(l2 - mx)
        inv = 1.0 / (w0 + w1 + w2)
        o = (w0 * inv) * og_scr[0, rows, :] + (w1 * inv) * og_scr[1, rows, :] + (w2 * inv) * og_scr[2, rows, :]
        o_ref[rows, :] = o.astype(o_ref.dtype)
        return carry

    lax.fori_loop(0, seq // chunk, merge_body, 0)


def swa_prompt_attention(qkv, slopes_tab, batch, seq):
    _, m, d = qkv.shape
    heads = d // HEAD_DIM
    specs = [pl.BlockSpec((None, SUBLANES, LANES), lambda b, h: (h, 0, 0))]
    for s in range(9):
        specs.append(pl.BlockSpec((None, seq, HEAD_DIM), lambda b, h, s=s: (s, b, h)))
    return pl.pallas_call(
        functools.partial(_swa_prompt_kernel, seq=seq),
        grid=(batch, heads),
        in_specs=specs,
        out_specs=pl.BlockSpec((seq, HEAD_DIM), lambda b, h: (b, h)),
        out_shape=jax.ShapeDtypeStruct((m, d), BF16),
        scratch_shapes=[
            pltpu.VMEM((3, seq, HEAD_DIM), F32),
            pltpu.VMEM((3, seq, LANES), F32),
            pltpu.VMEM((3, 2, Q_BLOCK, Q_BLOCK), F32),
        ],
        compiler_params=_params("arbitrary", "arbitrary"),
        name="swa_prompt_attention",
    )(slopes_tab, *([qkv] * 9))


SHIFT_BLOCK_ROWS = 8192


def _shift_rows_kernel(x_ref, o_ref, *, shift):
    last = pl.program_id(0) == pl.num_programs(0) - 1

    @pl.when(jnp.logical_not(last))
    def _():
        o_ref[...] = x_ref[...]

    @pl.when(last)
    def _():
        rows = o_ref.shape[0]
        o_ref[0:rows - shift, :] = x_ref[shift:rows, :]
        o_ref[rows - shift:rows, :] = x_ref[rows - shift:rows, :]


def swa_shift_window(cache, t_new):
    layers, dec_batch, win, heads, hd = cache.shape
    total = layers * dec_batch * win * heads
    shift = t_new * heads
    rows = min(SHIFT_BLOCK_ROWS, total)
    assert rows % shift == 0 and total % rows == 0
    per_block, last_start = rows // shift, (total - rows) // shift
    out = pl.pallas_call(
        functools.partial(_shift_rows_kernel, shift=shift),
        grid=(total // rows,),
        in_specs=[pl.BlockSpec((pl.Element(rows), pl.Element(hd)),
                               lambda i: (jnp.minimum(i * per_block + 1, last_start) * shift, 0))],
        out_specs=pl.BlockSpec((rows, hd), lambda i: (i, 0)),
        out_shape=jax.ShapeDtypeStruct((total, hd), cache.dtype),
        compiler_params=_params("arbitrary"),
        name="swa_shift_window",
    )(cache.reshape(total, hd))
    return out.reshape(cache.shape)


def _new_rows_kernel(*refs, n):
    news, outs = refs[:n], refs[2 * n:]
    for c in range(n):
        outs[c][...] = news[c][...]


def swa_write_new_rows(news, bufs, layer, t_new):
    n = len(bufs)
    dec_batch, _, heads, hd = news[0].shape
    specs = [pl.BlockSpec((None, t_new, heads, hd), lambda b: (b, 0, 0, 0))] * n
    specs += [pl.BlockSpec(memory_space=pl.ANY)] * n
    out_specs = [pl.BlockSpec((None, None, t_new, heads, hd),
                              lambda b, last=buf.shape[2] // t_new - 1: (layer, b, last, 0, 0)) for buf in bufs]
    return pl.pallas_call(
        functools.partial(_new_rows_kernel, n=n),
        grid=(dec_batch,),
        in_specs=specs,
        out_specs=out_specs,
        out_shape=[jax.ShapeDtypeStruct(b.shape, b.dtype) for b in bufs],
        input_output_aliases={n + c: c for c in range(n)},
        compiler_params=_params("arbitrary"),
        name="swa_write_new_rows",
    )(*news, *bufs)


def _swa_sample_kernel(sl_ref, r_ref, k0, v0, k1, v1, k2, v2, o_ref, m_scr, l_scr, acc_scr,
                       *, t_new, heads, past_len):
    step = pl.program_id(1)
    scale = HEAD_DIM ** -0.5
    chunks = k2.shape[0]
    near = ((k0, v0), (k1, v1))

    def update(state, q_row, k, v, slope, dist, valid):
        m, l, acc = state
        s = jnp.sum(k * q_row, axis=-1, keepdims=True) * scale - slope * dist.astype(F32)
        s = jnp.where(valid, s, MASK_VALUE)
        m_new = jnp.maximum(m, jnp.max(s, axis=0, keepdims=True))
        p = jnp.exp(s - m_new)
        alpha = jnp.exp(m - m_new)
        l = l * alpha + jnp.sum(p, axis=0, keepdims=True)
        acc = acc * alpha + jnp.sum(p * v, axis=0, keepdims=True)
        return m_new, l, acc

    def far_piece(state, h, t):
        win, dil = SWA_WINDOWS[2], SWA_DILATIONS[2]
        i = step * chunks + lax.broadcasted_iota(jnp.int32, (chunks, 1), 0)
        dist = win - dil * i
        valid = (dist > 0) & (dist <= SWA_SPAN * dil) & ((past_len - win) + t + dil * i >= 0)
        q_row = r_ref[2, 0, h, t:t + 1, :]
        slope = sl_ref[h, 2:3, 0:1]
        return update(state, q_row, k2[:, t * heads + h, :], v2[:, t * heads + h, :], slope, dist, valid)

    def first_body(h, carry):
        slabs = {}
        for g, (k_ref, v_ref) in enumerate(near):
            dil = SWA_DILATIONS[g]
            for first in range(min(dil, t_new)):
                rows = pl.ds(first * heads + h, SWA_SPAN, stride=dil * heads)
                slabs[g, first] = (k_ref[rows, :], v_ref[rows, :])
        for t in range(t_new):
            state = (jnp.full((1, 1), MASK_VALUE, F32), jnp.zeros((1, 1), F32), jnp.zeros((1, HEAD_DIM), F32))
            u = lax.broadcasted_iota(jnp.int32, (t_new, 1), 0)
            for g, dil in enumerate(SWA_DILATIONS):
                dist = t - u
                valid = (dist >= 0) & ((dist & (dil - 1)) == 0)
                state = update(state, r_ref[g, 0, h, t:t + 1, :], r_ref[g, 1, h], r_ref[g, 2, h],
                               sl_ref[h, g:g + 1, 0:1], dist, valid)
            i = lax.broadcasted_iota(jnp.int32, (SWA_SPAN, 1), 0)
            for g in range(len(near)):
                win, dil = SWA_WINDOWS[g], SWA_DILATIONS[g]
                first = t % dil
                tok = first + dil * i
                dist = win + t - tok
                valid = (dist > 0) & (dist <= SWA_SPAN * dil) & ((past_len - win) + tok >= 0)
                state = update(state, r_ref[g, 0, h, t:t + 1, :], *slabs[g, first],
                               sl_ref[h, g:g + 1, 0:1], dist, valid)
            m, l, acc = far_piece(state, h, t)
            m_scr[h, t:t + 1, :] = jnp.broadcast_to(m, (1, LANES))
            l_scr[h, t:t + 1, :] = jnp.broadcast_to(l, (1, LANES))
            acc_scr[h, t:t + 1, :] = acc
        return carry

    def last_body(h, carry):
        for t in range(t_new):
            state = (m_scr[h, t:t + 1, 0:1], l_scr[h, t:t + 1, 0:1], acc_scr[h, t:t + 1, :])
            m, l, acc = far_piece(state, h, t)
            o_ref[h, t:t + 1, :] = acc / l
        return carry

    @pl.when(step == 0)
    def _():
        lax.fori_loop(0, heads, first_body, 0)

    @pl.when(step == 1)
    def _():
        lax.fori_loop(0, heads, last_body, 0)


def swa_sample_attention(qkv_new, caches, layer, slopes_tab, dec_batch, t_new, past_len):
    _, m, d = qkv_new.shape
    heads = d // HEAD_DIM
    far_dil, far_win = SWA_DILATIONS[2], SWA_WINDOWS[2]
    assert 2 * t_new <= far_dil and SWA_DILATIONS[0] == 1 and t_new <= 2 * SWA_DILATIONS[1]
    r = qkv_new.reshape(3, 3, dec_batch, t_new, heads, HEAD_DIM).transpose(0, 1, 2, 4, 3, 5)
    near = [c.reshape(c.shape[0], dec_batch, c.shape[2] * heads, HEAD_DIM) for c in caches[:4]]
    n_chunks = far_win // far_dil
    far = [c.reshape(c.shape[0], dec_batch, n_chunks, far_dil * heads, HEAD_DIM) for c in caches[4:]]
    steps = 2
    specs = [
        pl.BlockSpec((heads, SUBLANES, LANES), lambda b, s: (0, 0, 0)),
        pl.BlockSpec((3, 3, None, heads, t_new, HEAD_DIM), lambda b, s: (0, 0, b, 0, 0, 0)),
    ]
    for c in near:
        specs.append(pl.BlockSpec((None, None, c.shape[2], HEAD_DIM), lambda b, s: (layer, b, 0, 0)))
    for c in far:
        specs.append(pl.BlockSpec((None, None, n_chunks // steps, far_dil * heads // 2, HEAD_DIM),
                                  lambda b, s: (layer, b, s, 0, 0)))
    o = pl.pallas_call(
        functools.partial(_swa_sample_kernel, t_new=t_new, heads=heads, past_len=past_len),
        grid=(dec_batch, steps),
        in_specs=specs,
        out_specs=pl.BlockSpec((None, heads, t_new, HEAD_DIM), lambda b, s: (b, 0, 0, 0)),
        out_shape=jax.ShapeDtypeStruct((dec_batch, heads, t_new, HEAD_DIM), F32),
        scratch_shapes=[pltpu.VMEM((heads, t_new, LANES), F32), pltpu.VMEM((heads, t_new, LANES), F32),
                        pltpu.VMEM((heads, t_new, HEAD_DIM), F32)],
        compiler_params=_params("arbitrary", "arbitrary"),
        name="swa_sample_attention",
    )(slopes_tab, r, *near, *far)
    return o.transpose(0, 2, 1, 3).reshape(m, d)


def _sb_prompt_kernel(b_ref, q_ref, k_ref, v_ref, o_ref, *, tq, hps):
    qb = pl.program_id(2)
    row = lax.broadcasted_iota(jnp.int32, (tq, tq), 0)
    col = lax.broadcasted_iota(jnp.int32, (tq, tq), 1)
    newer_mat = jnp.where(row > col, 1.0, 0.0).astype(BF16)
    causal = col < row

    def key_block(kb, carry, acc, masked):
        heads = range(hps)
        cols = [slice(h * HEAD_DIM, (h + 1) * HEAD_DIM) for h in heads]
        ks = pl.ds(pl.multiple_of(kb * tq, tq), tq)
        q = [q_ref[:, c].astype(BF16) for c in cols]
        k = [k_ref[ks, c].astype(BF16) for c in cols]
        v = [v_ref[ks, c].astype(BF16) for c in cols]
        z = [_dot(q[h], k[h], NT_DIMS) + b_ref[h:h + 1, :] for h in heads]
        log_keep = [-_softplus(z[h]) for h in heads]
        log_beta = [z[h] + log_keep[h] for h in heads]
        if masked:
            log_keep = [jnp.where(causal, lk, 0.0) for lk in log_keep]
        split = [_split_bf16(lk) for lk in log_keep]
        newer = [_dot(split[h][0], newer_mat) + _dot(split[h][1], newer_mat) + carry[h] for h in heads]
        a = [jnp.exp(log_beta[h] + newer[h]) for h in heads]
        if masked:
            a = [jnp.where(causal, x, 0.0) for x in a]
        acc = tuple(acc[h] + _dot(a[h].astype(BF16), v[h]) for h in heads)
        carry = tuple(carry[h] + jnp.sum(log_keep[h], axis=-1, keepdims=True) for h in heads)
        return carry, acc

    carry0 = tuple(jnp.zeros((tq, 1), F32) for _ in range(hps))
    acc0 = tuple(jnp.zeros((tq, HEAD_DIM), F32) for _ in range(hps))
    state = key_block(qb, carry0, acc0, True)
    state = lax.fori_loop(0, qb, lambda it, st: key_block(qb - 1 - it, st[0], st[1], False), state)
    for h in range(hps):
        o_ref[:, h * HEAD_DIM:(h + 1) * HEAD_DIM] = state[1][h].astype(o_ref.dtype)


def sb_prompt_attention(qkv, bias, batch, seq, tq, hps):
    _, m, d = qkv.shape
    heads = d // HEAD_DIM
    nq = seq // tq
    width = hps * HEAD_DIM
    bias_tab = _lane_table(bias.astype(F32).reshape(heads // hps, hps), tq)
    return pl.pallas_call(
        functools.partial(_sb_prompt_kernel, tq=tq, hps=hps),
        grid=(batch, heads // hps, nq),
        in_specs=[
            pl.BlockSpec((None, SUBLANES, tq), lambda b, h, i: (h, 0, 0)),
            pl.BlockSpec((None, tq, width), lambda b, h, i: (0, b * nq + i, h)),
            pl.BlockSpec((None, seq, width), lambda b, h, i: (1, b, h)),
            pl.BlockSpec((None, seq, width), lambda b, h, i: (2, b, h)),
        ],
        out_specs=pl.BlockSpec((tq, width), lambda b, h, i: (b * nq + i, h)),
        out_shape=jax.ShapeDtypeStruct((m, d), BF16),
        compiler_params=_params("arbitrary", "arbitrary", "arbitrary"),
        name="sb_prompt_attention",
    )(bias_tab, qkv, qkv, qkv)


def _sb_sample_kernel(pt_ref, qbd_ref, bias_ref, kn_ref, vn_ref, *refs, pages_per_step, t_new, heads):
    del pt_ref
    k_refs = refs[:pages_per_step]
    v_refs = refs[pages_per_step:2 * pages_per_step]
    o_ref = refs[2 * pages_per_step]
    carry_scr = refs[2 * pages_per_step + 1]
    step = pl.program_id(1)
    qbd = qbd_ref[...].astype(BF16)
    bias = bias_ref[...]
    n_col = heads * t_new
    key = lax.broadcasted_iota(jnp.int32, (PAGE_SIZE, n_col), 0)
    tok = lax.broadcasted_iota(jnp.int32, (PAGE_SIZE, n_col), 1) % t_new
    row = lax.broadcasted_iota(jnp.int32, (PAGE_SIZE, PAGE_SIZE), 0)
    col = lax.broadcasted_iota(jnp.int32, (PAGE_SIZE, PAGE_SIZE), 1)
    newer_mat = jnp.where(col > row, 1.0, 0.0).astype(BF16)

    def pages(loaders, masked):
        n = len(loaders)
        k = [jnp.concatenate([hk(h) for h in range(heads)], axis=1).astype(BF16) for hk, _ in loaders]
        z = [_dot(k[i], qbd) + bias for i in range(n)]
        log_keep = [-_softplus(x) for x in z]
        log_beta = [z[i] + log_keep[i] for i in range(n)]
        if masked:
            causal = key < tok
            log_keep = [jnp.where(causal, lk, 0.0) for lk in log_keep]
        split = [_split_bf16(lk) for lk in log_keep]
        within = [_dot(newer_mat, hi) + _dot(newer_mat, lo) for hi, lo in split]
        carry = carry_scr[...]
        a_t = []
        for i in range(n):
            a = jnp.exp(log_beta[i] + (within[i] + carry))
            if masked:
                a = jnp.where(causal, a, 0.0)
            a_t.append(a.T)
            carry = carry + jnp.sum(log_keep[i], axis=0, keepdims=True)
        carry_scr[...] = carry
        for h in range(heads):
            rows = slice(h * t_new, (h + 1) * t_new)
            o = _dot(a_t[0][rows, :].astype(BF16), loaders[0][1](h).astype(BF16))
            for i in range(1, n):
                o = o + _dot(a_t[i][rows, :].astype(BF16), loaders[i][1](h).astype(BF16))
            o_ref[:, h * HEAD_DIM:(h + 1) * HEAD_DIM] += o

    def head_rows(ref):
        return lambda h: ref[pl.ds(h, PAGE_SIZE, stride=heads), :]

    @pl.when(step == 0)
    def _():
        o_ref[...] = jnp.zeros_like(o_ref)
        carry_scr[...] = jnp.zeros_like(carry_scr)
        pages([(lambda h: kn_ref[:, h * HEAD_DIM:(h + 1) * HEAD_DIM],
                lambda h: vn_ref[:, h * HEAD_DIM:(h + 1) * HEAD_DIM])], True)

    pages([(head_rows(k_refs[p]), head_rows(v_refs[p])) for p in range(pages_per_step)], False)


def sb_sample_attention(q_new, k_new, v_new, cache_k, cache_v, layer, page_table, bias, t_new, pages_per_step=8):
    m, d = q_new.shape
    heads = d // HEAD_DIM
    dec_batch, n_pages = page_table.shape
    n_col = heads * t_new
    q4 = q_new.reshape(dec_batch, t_new, heads, HEAD_DIM)
    qbd = jnp.einsum('bthe,hg->bhegt', q4, jnp.eye(heads, dtype=F32)).reshape(dec_batch, d, n_col)
    bias_cols = jnp.repeat(bias.astype(F32), t_new).reshape(1, n_col)
    pad = ((0, 0), (0, PAGE_SIZE - t_new), (0, 0))
    kn = jnp.pad(k_new.reshape(dec_batch, t_new, d), pad)
    vn = jnp.pad(v_new.reshape(dec_batch, t_new, d), pad)
    steps = n_pages // pages_per_step

    def pool_rows(c):
        return c.reshape(c.shape[0], c.shape[1], PAGE_SIZE * heads, HEAD_DIM)

    def page_spec(p):
        return pl.BlockSpec(
            (None, None, PAGE_SIZE * heads, HEAD_DIM),
            lambda b, s, pt, p=p: (layer, pt[b, n_pages - 1 - (s * pages_per_step + p)], 0, 0))

    grid_spec = pltpu.PrefetchScalarGridSpec(
        num_scalar_prefetch=1,
        grid=(dec_batch, steps),
        in_specs=[
            pl.BlockSpec((None, d, n_col), lambda b, s, pt: (b, 0, 0)),
            pl.BlockSpec((1, n_col), lambda b, s, pt: (0, 0)),
            pl.BlockSpec((None, PAGE_SIZE, d), lambda b, s, pt: (b, 0, 0)),
            pl.BlockSpec((None, PAGE_SIZE, d), lambda b, s, pt: (b, 0, 0)),
        ] + [page_spec(p) for p in range(pages_per_step)] * 2,
        out_specs=pl.BlockSpec((t_new, d), lambda b, s, pt: (b, 0)),
        scratch_shapes=[pltpu.VMEM((1, n_col), F32)],
    )
    return pl.pallas_call(
        functools.partial(_sb_sample_kernel, pages_per_step=pages_per_step, t_new=t_new, heads=heads),
        grid_spec=grid_spec,
        out_shape=jax.ShapeDtypeStruct((m, d), F32),
        compiler_params=_params("arbitrary", "arbitrary"),
        name="sb_sample_attention",
    )(page_table, qbd, bias_cols, kn, vn, *([pool_rows(cache_k)] * pages_per_step),
      *([pool_rows(cache_v)] * pages_per_step))


def _head_rows_kernel(*refs, heads):
    xs, o_ref = refs[:-1], refs[-1]
    layer = pl.program_id(0)
    tm = xs[0].shape[0]
    for l, x_ref in enumerate(xs):
        @pl.when(layer == l)
        def _(x_ref=x_ref):
            for h in range(heads):
                o_ref[pl.ds(h, tm, stride=heads), :] = x_ref[:, h * HEAD_DIM:(h + 1) * HEAD_DIM]


def stack_head_rows(xs, seg, batch, seq, win):
    _, _, d = xs[0].shape
    heads = d // HEAD_DIM
    tm = min(512, win)
    assert win <= seq and win % tm == 0 and (seq - win) % tm == 0
    per_seq, per_win, first = seq // tm, win // tm, (seq - win) // tm
    nt = batch * per_win

    def in_spec(l):
        def index(lay, t):
            t = jnp.where(lay == l, t, jnp.where(lay < l, 0, nt - 1))
            return seg, (t // per_win) * per_seq + first + t % per_win, 0
        return pl.BlockSpec((None, tm, d), index)

    out = pl.pallas_call(
        functools.partial(_head_rows_kernel, heads=heads),
        grid=(len(xs), nt),
        in_specs=[in_spec(l) for l in range(len(xs))],
        out_specs=pl.BlockSpec((None, tm * heads, HEAD_DIM), lambda lay, t: (lay, t, 0)),
        out_shape=jax.ShapeDtypeStruct((len(xs), batch * win * heads, HEAD_DIM), xs[0].dtype),
        compiler_params=_params("arbitrary", "arbitrary"),
        name="stack_head_rows",
    )(*xs)
    return out.reshape(len(xs), batch, win, heads, HEAD_DIM)


def _alibi_slopes(heads):
    n = len(SWA_DILATIONS) * heads
    s = 2.0 ** (-8.0 * np.arange(1, n + 1) / n)
    return jnp.asarray(s.reshape(len(SWA_DILATIONS), heads), dtype=F32)


def _lane_table(per_head, width):
    heads, n = per_head.shape
    tab = jnp.pad(per_head, ((0, 0), (0, SUBLANES - n)))
    return jnp.broadcast_to(tab[:, :, None], (heads, SUBLANES, width))


def _last_rows(t, n):
    seq = t.shape[1]
    if seq >= n:
        return t[:, seq - n:]
    return jnp.pad(t, ((0, 0), (n - seq, 0), (0, 0), (0, 0)))


def kernel(x_prompt, x_sample, cache_swa_k0, cache_swa_v0, cache_swa_k1, cache_swa_v1, cache_swa_k2, cache_swa_v2, cache_sb_k, cache_sb_v, page_table, c_prompt, c_sample, w_ada, b_ada, norm_mix_g, norm_ffn_g, w_qkv_swa, w_o_swa, w_qkv_sb, w_o_sb, sb_bias, w_gate, w_up, w_down, final_norm_g):
    batch, seq, d = x_prompt.shape
    dec_batch, t_new, _ = x_sample.shape
    depth = w_ada.shape[0]
    heads = d // HEAD_DIM
    n_pages = page_table.shape[1]
    past_len = n_pages * cache_sb_k.shape[2]
    mp_rows, ms_rows = batch * seq, dec_batch * t_new
    tm_p, tm_s = 1024, ms_rows
    sb_tq, sb_hps = 256, 8

    slopes_tab = _lane_table(_alibi_slopes(heads).T, LANES)
    swa_caches = (cache_swa_k0, cache_swa_v0, cache_swa_k1, cache_swa_v1, cache_swa_k2, cache_swa_v2)
    swa_s = [swa_shift_window(c, t_new) for c in swa_caches]

    n_c = batch + dec_batch
    c_rows = -(-n_c // SUBLANES) * SUBLANES
    c_all = jnp.pad(jnp.concatenate([c_prompt, c_sample], axis=0), ((0, c_rows - n_c), (0, 0)))
    mods = ada_all(c_all, w_ada, b_ada)

    xp = x_prompt.reshape(mp_rows, d)
    xs = x_sample.reshape(ms_rows, d)
    swa_qkv_p, sb_qkv_p = [], []
    sb_s = [[], []]

    for i in range(depth):
        mod_i = mods[i].reshape(c_rows, 6, d)
        mp = [mod_i[:batch, j][:, None, :] for j in range(6)]
        ms = [jnp.repeat(mod_i[batch:n_c, j], t_new, axis=0)[None] for j in range(6)]
        hs = norm_mod(xs, norm_mix_g[i], ms[0], ms[1], tm_s)
        if i % 2 == 0:
            a = i // 2
            qkv_p = norm_project_segments(xp, norm_mix_g[i], mp[0], mp[1], w_qkv_swa, a, d, tm_p, 1024)
            qkv_s = project_segments(hs, w_qkv_swa, a, d, tm_s, 1024)
            op = swa_prompt_attention(qkv_p, slopes_tab, batch, seq)
            os_ = swa_sample_attention(qkv_s, swa_caches, a, slopes_tab, dec_batch, t_new, past_len)
            news = [qkv_s[3 * g + 1 + c].reshape(dec_batch, t_new, heads, HEAD_DIM)
                    for g in range(len(SWA_WINDOWS)) for c in range(2)]
            swa_s = swa_write_new_rows(news, swa_s, a, t_new)
            swa_qkv_p.append(qkv_p)
            w_o, la = w_o_swa, a
        else:
            b = i // 2
            qkv_p = norm_project_segments(xp, norm_mix_g[i], mp[0], mp[1], w_qkv_sb, b, d, tm_p, 1024,
                                          scaled_seg0=HEAD_DIM ** -0.5)
            qkv_s = project_segments(hs, w_qkv_sb, b, d, tm_s, 1024, scaled_seg0=HEAD_DIM ** -0.5)
            op = sb_prompt_attention(qkv_p, sb_bias[b], batch, seq, sb_tq, sb_hps)
            os_ = sb_sample_attention(qkv_s[0], qkv_s[1], qkv_s[2], cache_sb_k, cache_sb_v, b,
                                      page_table, sb_bias[b], t_new)
            sb_qkv_p.append(qkv_p)
            for c in range(2):
                sb_s[c].append(qkv_s[1 + c].reshape(dec_batch, t_new, heads, HEAD_DIM))
            w_o, la = w_o_sb, b
        xp = project_residual(op, w_o, la, xp, mp[2], tm_p, 1024)
        xs = project_residual(os_, w_o, la, xs, ms[2], tm_s, 1024)
        hs = norm_mod(xs, norm_ffn_g[i], ms[3], ms[4], tm_s)
        ap = norm_project_swiglu(xp, norm_ffn_g[i], mp[3], mp[4], w_gate, w_up, i, tm_p, 512)
        as_ = project_swiglu(hs, w_gate, w_up, i, tm_s, 512)
        xp = project_residual(ap, w_down, i, xp, mp[5], tm_p, 256)
        xs = project_residual(as_, w_down, i, xs, ms[5], tm_s, 256)

    y_prompt = final_norm(xp, final_norm_g, tm_p).reshape(batch, seq, d)
    y_sample = final_norm(xs, final_norm_g, tm_s).reshape(dec_batch, t_new, d)
    outs = [y_prompt, y_sample]
    for g, win in enumerate(SWA_WINDOWS):
        for c in range(2):
            seg = 3 * g + 1 + c
            if win <= seq:
                outs.append(stack_head_rows(swa_qkv_p, seg, batch, seq, win))
            else:
                outs.append(jnp.stack([_last_rows(q[seg].reshape(batch, seq, heads, HEAD_DIM), win)
                                       for q in swa_qkv_p]))
    outs += [stack_head_rows(sb_qkv_p, 1 + c, batch, seq, seq) for c in range(2)]
    outs += list(swa_s)
    outs += [jnp.stack(t) for t in sb_s]
    return tuple(outs)
```
